```python
import math
import jax, jax.numpy as jnp
from jax import lax
import numpy as np

D_MODEL = 1024
BATCH = 1
SEQ = 16384
DEPTH = 2

ATT_HEADS = 8
ATT_KV_HEADS = 2
HEAD_DIM = 64
WINDOW = 128
ROT_DIM = HEAD_DIM // 4
ROPE_THETA = 500000.0
SGU_GROUPS = 8
SGU_GROUP_DIM = 64
CHUNK = 128
CONV_WIDTH = 3
CONV_DIM = D_MODEL
D_FF = 2816
PLE_DIM = 256

N_EVEN = (DEPTH + 1) // 2
N_ODD = DEPTH // 2
ALPHA = (2 * DEPTH) ** 0.25
BETA = (8 * DEPTH) ** -0.25
LN_EPS = 1e-5

Q_W = ATT_HEADS * HEAD_DIM
KV_W = ATT_KV_HEADS * HEAD_DIM
SGU_W = SGU_GROUPS * SGU_GROUP_DIM
AB_IN = Q_W + 2 * KV_W + 2 * SGU_W
AB_OUT = Q_W + SGU_W

kernel_name = "hybrid_gmlp_swa_shortconv_deepnorm"


def layer_norm(x, g, b):
    xf = x.astype(jnp.float32)
    mu = jnp.mean(xf, axis=-1, keepdims=True)
    var = jnp.mean(jnp.square(xf - mu), axis=-1, keepdims=True)
    y = (xf - mu) * lax.rsqrt(var + LN_EPS)
    return (y * g.astype(jnp.float32) + b.astype(jnp.float32)).astype(x.dtype)


def swiglu(x, w_gate, w_up, w_down):
    return (jax.nn.silu(x @ w_gate) * (x @ w_up)) @ w_down


def partial_rotary(x, positions):
    half = ROT_DIM // 2
    inv = jnp.power(ROPE_THETA, -jnp.arange(half, dtype=jnp.float32) * (2.0 / ROT_DIM))
    ang = positions.astype(jnp.float32)[..., None] * inv
    cos = jnp.cos(ang)[:, :, None, :]
    sin = jnp.sin(ang)[:, :, None, :]
    xf = x.astype(jnp.float32)
    x1 = xf[..., :half]
    x2 = xf[..., half:ROT_DIM]
    rot = jnp.concatenate([x1 * cos - x2 * sin, x2 * cos + x1 * sin, xf[..., ROT_DIM:]], axis=-1)
    return rot.astype(x.dtype)


def sliding_window_sink_attention(q, k, v, sinks):
    b, s = q.shape[0], q.shape[1]
    nb = s // WINDOW
    g = ATT_HEADS // ATT_KV_HEADS
    qb = q.reshape(b, nb, WINDOW, ATT_KV_HEADS, g, HEAD_DIM)

    def band(t):
        tb = t.reshape(b, nb, WINDOW, ATT_KV_HEADS, HEAD_DIM)
        prev = jnp.pad(tb, ((0, 0), (1, 0), (0, 0), (0, 0), (0, 0)))[:, :-1]
        return jnp.concatenate([prev, tb], axis=2)

    kb, vb = band(k), band(v)
    scores = jnp.einsum('bnqhgd,bnkhd->bnhgqk', qb, kb).astype(jnp.float32) * (HEAD_DIM ** -0.5)
    qi = jnp.arange(WINDOW)[:, None]
    kj = jnp.arange(2 * WINDOW)[None, :]
    diff = qi + WINDOW - kj
    blk = jnp.arange(nb)[:, None, None]
    valid = (diff >= 0) & (diff < WINDOW) & (blk * WINDOW + kj - WINDOW >= 0)
    scores = jnp.where(valid[None, :, None, None], scores, -1e30)
    sink = jnp.broadcast_to(sinks.astype(jnp.float32).reshape(1, 1, ATT_KV_HEADS, g, 1, 1),
                            scores.shape[:-1] + (1,))
    probs = jax.nn.softmax(jnp.concatenate([scores, sink], axis=-1), axis=-1)[..., :-1]
    out = jnp.einsum('bnhgqk,bnkhd->bnqhgd', probs.astype(vb.dtype), vb)
    return out.reshape(b, s, ATT_HEADS * HEAD_DIM)


def chunked_sgu(u, v, ln_g, ln_b, w_s, b_s):
    b, s = u.shape[0], u.shape[1]
    nc = s // CHUNK
    u = jax.nn.gelu(u)
    v = layer_norm(jax.nn.gelu(v), ln_g, ln_b)
    vc = v.reshape(b, nc, CHUNK, SGU_GROUPS, SGU_GROUP_DIM)
    w = w_s * jnp.tril(jnp.ones((CHUNK, CHUNK), dtype=w_s.dtype))
    mixed = jnp.einsum('gts,bcsgd->bctgd', w, vc) + jnp.transpose(b_s)[None, None, :, :, None]
    return u * mixed.reshape(b, s, SGU_W)


def short_conv_mixer(x, w_in, conv_w, w_out):
    s = x.shape[1]
    gate_b, gate_c, z = jnp.split(x @ w_in, 3, axis=-1)
    zp = jnp.pad(gate_c * z, ((0, 0), (CONV_WIDTH - 1, 0), (0, 0)))
    y = conv_w[0] * zp[:, 0:s]
    for t in range(1, CONV_WIDTH):
        y = y + conv_w[t] * zp[:, t:t + s]
    return (gate_b * y) @ w_out


def setup_inputs(seed: int = 0) -> dict:
    key = jax.random.key(seed)
    ks = jax.random.split(key, 24)
    nrm = jax.random.normal
    f32 = jnp.float32
    x = nrm(ks[0], (BATCH, SEQ, D_MODEL), f32)
    p = nrm(ks[1], (DEPTH, BATCH, SEQ, PLE_DIM), f32)
    offset = jax.random.randint(ks[2], (BATCH, 1), 0, 4096, dtype=jnp.int32)
    positions = offset + jnp.arange(SEQ, dtype=jnp.int32)[None, :]
    ln_g = 1.0 + 0.02 * nrm(ks[3], (DEPTH, 3, D_MODEL), f32)
    ln_b = 0.02 * nrm(ks[4], (DEPTH, 3, D_MODEL), f32)
    ffn_w_gate = nrm(ks[5], (DEPTH, 2, D_MODEL, D_FF), f32) * D_MODEL ** -0.5
    ffn_w_up = nrm(ks[6], (DEPTH, 2, D_MODEL, D_FF), f32) * D_MODEL ** -0.5
    ffn_w_down = nrm(ks[7], (DEPTH, 2, D_FF, D_MODEL), f32) * (D_FF ** -0.5 * BETA)
    ab_w_in = nrm(ks[8], (N_EVEN, D_MODEL, AB_IN), f32) * D_MODEL ** -0.5
    ab_sinks = nrm(ks[9], (N_EVEN, ATT_HEADS), f32)
    sgu_ln_g = 1.0 + 0.02 * nrm(ks[10], (N_EVEN, SGU_W), f32)
    sgu_ln_b = 0.02 * nrm(ks[11], (N_EVEN, SGU_W), f32)
    sgu_w_s = nrm(ks[12], (N_EVEN, SGU_GROUPS, CHUNK, CHUNK), f32) * CHUNK ** -0.5
    sgu_b_s = 1.0 + 0.1 * nrm(ks[13], (N_EVEN, SGU_GROUPS, CHUNK), f32)
    ab_w_out = nrm(ks[14], (N_EVEN, AB_OUT, D_MODEL), f32) * (AB_OUT ** -0.5 * BETA)
    sc_w_in = nrm(ks[15], (N_ODD, D_MODEL, 3 * CONV_DIM), f32) * D_MODEL ** -0.5
    sc_conv_w = nrm(ks[16], (N_ODD, CONV_WIDTH, CONV_DIM), f32) * CONV_WIDTH ** -0.5
    sc_w_out = nrm(ks[17], (N_ODD, CONV_DIM, D_MODEL), f32) * (CONV_DIM ** -0.5 * BETA)
    ple_w_proj = nrm(ks[18], (DEPTH, PLE_DIM, D_MODEL), f32) * PLE_DIM ** -0.5
    ple_w_gate = nrm(ks[19], (DEPTH, D_MODEL, D_MODEL), f32) * D_MODEL ** -0.5
    return {"x": x, "p": p, "positions": positions, "ln_g": ln_g, "ln_b": ln_b,
            "ffn_w_gate": ffn_w_gate, "ffn_w_up": ffn_w_up, "ffn_w_down": ffn_w_down,
            "ab_w_in": ab_w_in, "ab_sinks": ab_sinks, "sgu_ln_g": sgu_ln_g, "sgu_ln_b": sgu_ln_b,
            "sgu_w_s": sgu_w_s, "sgu_b_s": sgu_b_s, "ab_w_out": ab_w_out,
            "sc_w_in": sc_w_in, "sc_conv_w": sc_conv_w, "sc_w_out": sc_w_out,
            "ple_w_proj": ple_w_proj, "ple_w_gate": ple_w_gate}


def reference(x, p, positions, ln_g, ln_b, ffn_w_gate, ffn_w_up, ffn_w_down,
              ab_w_in, ab_sinks, sgu_ln_g, sgu_ln_b, sgu_w_s, sgu_b_s, ab_w_out,
              sc_w_in, sc_conv_w, sc_w_out, ple_w_proj, ple_w_gate):
    b, s = x.shape[0], x.shape[1]
    g = ATT_HEADS // ATT_KV_HEADS
    splits = [Q_W, Q_W + KV_W, Q_W + 2 * KV_W, Q_W + 2 * KV_W + SGU_W]
    for i in range(DEPTH):
        ff1 = swiglu(x, ffn_w_gate[i, 0], ffn_w_up[i, 0], ffn_w_down[i, 0])
        x = layer_norm(ALPHA * x + 0.5 * ff1, ln_g[i, 0], ln_b[i, 0])
        j = i // 2
        if i % 2 == 0:
            h = x @ ab_w_in[j]
            q, k, v, su, sv = jnp.split(h, splits, axis=-1)
            q = partial_rotary(q.reshape(b, s, ATT_HEADS, HEAD_DIM), positions)
            q = q.reshape(b, s, ATT_KV_HEADS, g, HEAD_DIM)
            k = partial_rotary(k.reshape(b, s, ATT_KV_HEADS, HEAD_DIM), positions)
            v = v.reshape(b, s, ATT_KV_HEADS, HEAD_DIM)
            att = sliding_window_sink_attention(q, k, v, ab_sinks[j])
            sgu = chunked_sgu(su, sv, sgu_ln_g[j], sgu_ln_b[j], sgu_w_s[j], sgu_b_s[j])
            mix = jnp.concatenate([att, sgu], axis=-1) @ ab_w_out[j]
        else:
            mix = short_conv_mixer(x, sc_w_in[j], sc_conv_w[j], sc_w_out[j])
        x = layer_norm(ALPHA * x + mix, ln_g[i, 1], ln_b[i, 1])
        ff2 = swiglu(x, ffn_w_gate[i, 1], ffn_w_up[i, 1], ffn_w_down[i, 1])
        x = layer_norm(ALPHA * x + 0.5 * ff2, ln_g[i, 2], ln_b[i, 2])
        x = x + (p[i] @ ple_w_proj[i]) * jax.nn.sigmoid(x @ ple_w_gate[i])
    return x
```

```python
import functools
import math

import jax
import jax.numpy as jnp
from jax import lax
from jax.experimental import pallas as pl
from jax.experimental.pallas import tpu as pltpu

ATT_HEADS = 8
ATT_KV_HEADS = 2
HEAD_DIM = 64
WINDOW = 128
ROT_DIM = HEAD_DIM // 4
ROPE_THETA = 500000.0
SGU_GROUPS = 8
SGU_GROUP_DIM = 64
CHUNK = 128
CONV_WIDTH = 3
LN_EPS = 1e-5
NEG_INF = -1e30

Q_W = ATT_HEADS * HEAD_DIM
KV_W = ATT_KV_HEADS * HEAD_DIM
SGU_W = SGU_GROUPS * SGU_GROUP_DIM

LANES = 128
SUBLANES = 8
VMEM_LIMIT = 56 * 1024 * 1024

BF16 = jnp.bfloat16
F32 = jnp.float32


def _dot(a, b):
    return jnp.dot(a, b, preferred_element_type=F32)


def _layer_norm(y, g, b):
    mu = jnp.mean(y, axis=-1, keepdims=True)
    d = y - mu
    var = jnp.mean(d * d, axis=-1, keepdims=True)
    return d * lax.rsqrt(var + LN_EPS) * g + b


def _gelu_tanh(x):
    c = math.sqrt(2.0 / math.pi)
    return x * (0.5 * (1.0 + jnp.tanh(c * (x + 0.044715 * (x * x * x)))))


def _resident(shape):
    nd = len(shape)
    return pl.BlockSpec(shape, lambda i: (0,) * nd, pipeline_mode=pl.Buffered(1))


def _rows(tm, width):
    return pl.BlockSpec((tm, width), lambda i: (i, 0))


def _params():
    return pltpu.CompilerParams(dimension_semantics=("arbitrary",),
                                vmem_limit_bytes=VMEM_LIMIT)


def _ffn_body(alpha, x_ref, wg_ref, wu_ref, wd_ref, g_ref, b_ref):
    x = x_ref[...]
    xb = x.astype(BF16)
    gate = _dot(xb, wg_ref[...])
    up = _dot(xb, wu_ref[...])
    h = (gate * jax.nn.sigmoid(gate) * up).astype(BF16)
    ff = _dot(h, wd_ref[...])
    return _layer_norm(alpha * x + 0.5 * ff, g_ref[...], b_ref[...])


def _ffn_kernel(alpha, x_ref, wg_ref, wu_ref, wd_ref, g_ref, b_ref, o_ref):
    o_ref[...] = _ffn_body(alpha, x_ref, wg_ref, wu_ref, wd_ref, g_ref, b_ref)


def _ffn_ple_kernel(alpha, x_ref, wg_ref, wu_ref, wd_ref, g_ref, b_ref,
                    p_ref, wp_ref, wpg_ref, o_ref):
    y = _ffn_body(alpha, x_ref, wg_ref, wu_ref, wd_ref, g_ref, b_ref)
    emb = _dot(p_ref[...].astype(BF16), wp_ref[...])
    gate = jax.nn.sigmoid(_dot(y.astype(BF16), wpg_ref[...]))
    o_ref[...] = y + emb * gate


def _ffn_call(x, wg, wu, wd, g, b, alpha, tm, ple=None):
    n, d = x.shape
    f = wg.shape[1]
    in_specs = [_rows(tm, d), _resident((d, f)), _resident((d, f)), _resident((f, d)),
                _resident((1, d)), _resident((1, d))]
    args = [x, wg, wu, wd, g, b]
    if ple is None:
        body = functools.partial(_ffn_kernel, alpha)
        name = "ffn_ln"
    else:
        p, wp, wpg = ple
        pd = p.shape[1]
        in_specs += [_rows(tm, pd), _resident((pd, d)), _resident((d, d))]
        args += [p, wp, wpg]
        body = functools.partial(_ffn_ple_kernel, alpha)
        name = "ffn_ln_ple"
    return pl.pallas_call(
        body,
        grid=(n // tm,),
        in_specs=in_specs,
        out_specs=_rows(tm, d),
        out_shape=jax.ShapeDtypeStruct((n, d), F32),
        compiler_params=_params(),
        name=name,
    )(*args)


def _mixer_ab_kernel(alpha, tm, tiles_per_seq,
                     sinks_ref, x_ref, pos_ref, win_ref, inv_ref, sgn_ref,
                     sg_ref, sb_ref, ws_ref, bs_ref, wout_ref, g_ref, b_ref,
                     o_ref,
                     q_scr, kv_scr, u_scr, sv_scr, wcat_scr, cat_scr):
    i = pl.program_id(0)
    first_tile = (i % tiles_per_seq) == 0
    nblk = tm // WINDOW
    lane = lax.broadcasted_iota(jnp.int32, (1, LANES), 1)
    low_half = lane < HEAD_DIM

    @pl.when(i == 0)
    def _prepare_spatial_weights():
        r = lax.broadcasted_iota(jnp.int32, (CHUNK, CHUNK), 0)
        c = lax.broadcasted_iota(jnp.int32, (CHUNK, CHUNK), 1)
        tril = r >= c
        for j in range(SGU_GROUPS // 2):
            a = jnp.where(tril, ws_ref[2 * j], 0.0)
            bb = jnp.where(tril, ws_ref[2 * j + 1], 0.0)
            wcat_scr[j] = jnp.concatenate([a, bb], axis=1).astype(BF16)

    @pl.when(first_tile)
    def _reset_halo():
        kv_scr[0:WINDOW, :] = jnp.zeros((WINDOW, 4 * LANES), BF16)

    x = x_ref[...]
    h = _dot(x.astype(BF16), win_ref[...])

    ang = pos_ref[...].astype(F32) * inv_ref[...]
    cos_t = jnp.cos(ang)
    sin_t = jnp.sin(ang) * sgn_ref[...]
    take_up = (lane % HEAD_DIM) < (ROT_DIM // 2)

    def rotary(t):
        up = pltpu.roll(t, LANES - ROT_DIM // 2, axis=1)
        dn = pltpu.roll(t, ROT_DIM // 2, axis=1)
        return t * cos_t + jnp.where(take_up, up, dn) * sin_t

    scale = HEAD_DIM ** -0.5
    for j in range(Q_W // LANES):
        qj = rotary(h[:, j * LANES:(j + 1) * LANES]) * scale
        q_scr[:, j * LANES:(j + 1) * LANES] = qj.astype(BF16)

    def dup_heads(t):
        sw = pltpu.roll(t, HEAD_DIM, axis=1)
        return jnp.where(low_half, t, sw), jnp.where(low_half, sw, t)

    k0, k1 = dup_heads(rotary(h[:, Q_W:Q_W + KV_W]))
    v0, v1 = dup_heads(h[:, Q_W + KV_W:Q_W + 2 * KV_W])
    kv_scr[WINDOW:, 0 * LANES:1 * LANES] = k0.astype(BF16)
    kv_scr[WINDOW:, 1 * LANES:2 * LANES] = k1.astype(BF16)
    kv_scr[WINDOW:, 2 * LANES:3 * LANES] = v0.astype(BF16)
    kv_scr[WINDOW:, 3 * LANES:4 * LANES] = v1.astype(BF16)

    su0 = Q_W + 2 * KV_W
    u_scr[...] = _gelu_tanh(h[:, su0:su0 + SGU_W])
    sv = _layer_norm(_gelu_tanh(h[:, su0 + SGU_W:su0 + 2 * SGU_W]), sg_ref[...], sb_ref[...])
    sv_scr[...] = sv.astype(BF16)

    qi = lax.broadcasted_iota(jnp.int32, (2 * WINDOW, 2 * WINDOW), 0) % WINDOW
    kj = lax.broadcasted_iota(jnp.int32, (2 * WINDOW, 2 * WINDOW), 1)
    upper_ok = kj <= qi + WINDOW
    row_first = lax.broadcasted_iota(jnp.int32, (2 * WINDOW, 1), 0) < WINDOW

    def block(bi, carry):
        r0 = pl.multiple_of(bi * WINDOW, WINDOW)
        lower = jnp.where(jnp.logical_and(first_tile, bi == 0), WINDOW, qi + 1)
        valid = jnp.logical_and(upper_ok, kj >= lower)
        for kvh in range(ATT_KV_HEADS):
            kband = kv_scr[pl.ds(r0, 2 * WINDOW), kvh * LANES:(kvh + 1) * LANES]
            vband = kv_scr[pl.ds(r0, 2 * WINDOW), (2 + kvh) * LANES:(3 + kvh) * LANES]
            for pr in range(2):
                slab = kvh * 2 + pr
                qp = q_scr[pl.ds(r0, WINDOW), slab * LANES:(slab + 1) * LANES]
                zero = jnp.zeros_like(qp)
                qs = jnp.concatenate([jnp.where(low_half, qp, zero),
                                      jnp.where(low_half, zero, qp)], axis=0)
                s = lax.dot_general(qs, kband, (((1,), (1,)), ((), ())),
                                    preferred_element_type=F32)
                s = jnp.where(valid, s, NEG_INF)
                sink = jnp.where(row_first, sinks_ref[2 * slab], sinks_ref[2 * slab + 1])
                m = jnp.maximum(jnp.max(s, axis=-1, keepdims=True), sink)
                p = jnp.exp(s - m)
                denom = jnp.sum(p, axis=-1, keepdims=True) + jnp.exp(sink - m)
                o = _dot(p.astype(BF16), vband) / denom
                att = jnp.where(low_half, o[:WINDOW], o[WINDOW:])
                cat_scr[pl.ds(r0, WINDOW), slab * LANES:(slab + 1) * LANES] = att.astype(BF16)
        for j in range(SGU_GROUPS // 2):
            vp = sv_scr[pl.ds(r0, CHUNK), j * LANES:(j + 1) * LANES]
            zero = jnp.zeros_like(vp)
            rhs = jnp.concatenate([jnp.where(low_half, vp, zero),
                                   jnp.where(low_half, zero, vp)], axis=0)
            mixed = _dot(wcat_scr[j], rhs) + bs_ref[:, j * LANES:(j + 1) * LANES]
            out = u_scr[pl.ds(r0, CHUNK), j * LANES:(j + 1) * LANES] * mixed
            cat_scr[pl.ds(r0, CHUNK), Q_W + j * LANES:Q_W + (j + 1) * LANES] = out.astype(BF16)
        return carry

    lax.fori_loop(0, nblk, block, 0)

    kv_scr[0:WINDOW, :] = kv_scr[tm:tm + WINDOW, :]

    mix = _dot(cat_scr[...], wout_ref[...])
    o_ref[...] = _layer_norm(alpha * x + mix, g_ref[...], b_ref[...])


def _mixer_ab_call(x, pos, w_in, sinks, inv_lane, sgn_lane, sgu_g, sgu_b, w_s, bias_t,
                   w_out, g, b, alpha, tm, seq):
    n, d = x.shape
    ab_in = w_in.shape[1]
    body = functools.partial(_mixer_ab_kernel, alpha, tm, seq // tm)
    in_specs = [
        pl.BlockSpec(memory_space=pltpu.SMEM),
        _rows(tm, d),
        _rows(tm, 1),
        _resident((d, ab_in)),
        _resident((1, LANES)), _resident((1, LANES)),
        _resident((1, SGU_W)), _resident((1, SGU_W)),
        _resident((SGU_GROUPS, CHUNK, CHUNK)),
        _resident((CHUNK, SGU_W)),
        _resident((Q_W + SGU_W, d)),
        _resident((1, d)), _resident((1, d)),
    ]
    scratch = [
        pltpu.VMEM((tm, Q_W), BF16),
        pltpu.VMEM((tm + WINDOW, 4 * LANES), BF16),
        pltpu.VMEM((tm, SGU_W), F32),
        pltpu.VMEM((tm, SGU_W), BF16),
        pltpu.VMEM((SGU_GROUPS // 2, CHUNK, 2 * CHUNK), BF16),
        pltpu.VMEM((tm, Q_W + SGU_W), BF16),
    ]
    return pl.pallas_call(
        body,
        grid=(n // tm,),
        in_specs=in_specs,
        out_specs=_rows(tm, d),
        out_shape=jax.ShapeDtypeStruct((n, d), F32),
        scratch_shapes=scratch,
        compiler_params=_params(),
        name="mixer_attn_sgu",
    )(sinks, x, pos, w_in, inv_lane, sgn_lane, sgu_g, sgu_b, w_s, bias_t, w_out, g, b)


def _mixer_conv_kernel(alpha, tm, tiles_per_seq,
                       x_ref, win_ref, cw_ref, wout_ref, g_ref, b_ref, o_ref, cz_scr):
    i = pl.program_id(0)
    d = x_ref.shape[1]
    pad = SUBLANES

    @pl.when((i % tiles_per_seq) == 0)
    def _reset_halo():
        cz_scr[0:pad, :] = jnp.zeros((pad, d), F32)

    x = x_ref[...]
    h = _dot(x.astype(BF16), win_ref[...])
    cz = h[:, d:2 * d] * h[:, 2 * d:3 * d]
    cz_scr[pad:, :] = cz
    y = cw_ref[CONV_WIDTH - 1:CONV_WIDTH, :] * cz
    for t in range(CONV_WIDTH - 1):
        back = CONV_WIDTH - 1 - t
        y = y + cw_ref[t:t + 1, :] * cz_scr[pad - back:pad - back + tm, :]
    cz_scr[0:pad, :] = cz_scr[tm:tm + pad, :]
    mix = _dot((h[:, 0:d] * y).astype(BF16), wout_ref[...])
    o_ref[...] = _layer_norm(alpha * x + mix, g_ref[...], b_ref[...])


def _mixer_conv_call(x, w_in, conv_w, w_out, g, b, alpha, tm, seq):
    n, d = x.shape
    body = functools.partial(_mixer_conv_kernel, alpha, tm, seq // tm)
    return pl.pallas_call(
        body,
        grid=(n // tm,),
        in_specs=[_rows(tm, d), _resident((d, 3 * d)), _resident((CONV_WIDTH, d)),
                  _resident((d, d)), _resident((1, d)), _resident((1, d))],
        out_specs=_rows(tm, d),
        out_shape=jax.ShapeDtypeStruct((n, d), F32),
        scratch_shapes=[pltpu.VMEM((tm + SUBLANES, d), F32)],
        compiler_params=_params(),
        name="mixer_conv",
    )(x, w_in, conv_w, w_out, g, b)


def kernel(x, p, positions, ln_g, ln_b, ffn_w_gate, ffn_w_up, ffn_w_down, ab_w_in, ab_sinks,
           sgu_ln_g, sgu_ln_b, sgu_w_s, sgu_b_s, ab_w_out, sc_w_in, sc_conv_w, sc_w_out,
           ple_w_proj, ple_w_gate):
    bsz, seq, d = x.shape
    depth = p.shape[0]
    alpha = (2 * depth) ** 0.25
    tm = 512
    assert seq % tm == 0 and tm % WINDOW == 0
    assert sc_conv_w.shape[1] == CONV_WIDTH

    n = bsz * seq
    xs = x.reshape(n, d)
    pos = positions.reshape(n, 1)

    half = ROT_DIM // 2
    inv = jnp.power(ROPE_THETA, -jnp.arange(half, dtype=F32) * (2.0 / ROT_DIM))
    r = jnp.arange(LANES) % HEAD_DIM
    inv_lane = jnp.where(r < ROT_DIM, inv[r % half], 0.0).reshape(1, LANES).astype(F32)
    sgn_lane = jnp.where(r < half, -1.0, jnp.where(r < ROT_DIM, 1.0, 0.0)).reshape(1, LANES).astype(F32)

    row = lambda v: v.reshape(1, -1)
    for i in range(depth):
        j = i // 2
        xs = _ffn_call(xs, ffn_w_gate[i, 0].astype(BF16), ffn_w_up[i, 0].astype(BF16),
                       ffn_w_down[i, 0].astype(BF16), row(ln_g[i, 0]), row(ln_b[i, 0]), alpha, tm)
        if i % 2 == 0:
            bias_t = jnp.repeat(jnp.transpose(sgu_b_s[j]), SGU_GROUP_DIM, axis=1)
            xs = _mixer_ab_call(xs, pos, ab_w_in[j].astype(BF16), ab_sinks[j], inv_lane, sgn_lane,
                                row(sgu_ln_g[j]), row(sgu_ln_b[j]), sgu_w_s[j], bias_t,
                                ab_w_out[j].astype(BF16), row(ln_g[i, 1]), row(ln_b[i, 1]),
                                alpha, tm, seq)
        else:
            xs = _mixer_conv_call(xs, sc_w_in[j].astype(BF16), sc_conv_w[j],
                                  sc_w_out[j].astype(BF16), row(ln_g[i, 1]), row(ln_b[i, 1]),
                                  alpha, tm, seq)
        xs = _ffn_call(xs, ffn_w_gate[i, 1].astype(BF16), ffn_w_up[i, 1].astype(BF16),
                       ffn_w_down[i, 1].astype(BF16), row(ln_g[i, 2]), row(ln_b[i, 2]), alpha, tm,
                       ple=(p[i].reshape(n, -1), ple_w_proj[i].astype(BF16),
                            ple_w_gate[i].astype(BF16)))
    return xs.reshape(bsz, seq, d)
```

```python
import functools
import math

import jax
import jax.numpy as jnp
from jax import lax
from jax.experimental import pallas as pl
from jax.experimental.pallas import tpu as pltpu

ATT_HEADS = 8
ATT_KV_HEADS = 2
HEAD_DIM = 64
WINDOW = 128
ROT_DIM = HEAD_DIM // 4
ROPE_THETA = 500000.0
SGU_GROUPS = 8
SGU_GROUP_DIM = 64
CHUNK = 128
CONV_WIDTH = 3
LN_EPS = 1e-5
NEG_INF = -1e30

Q_W = ATT_HEADS * HEAD_DIM
KV_W = ATT_KV_HEADS * HEAD_DIM
SGU_W = SGU_GROUPS * SGU_GROUP_DIM

LANES = 128
SUBLANES = 8
VMEM_LIMIT = 56 * 1024 * 1024
STAGE_BYTES = 1024 * 1024

BF16 = jnp.bfloat16
F32 = jnp.float32


def _dot(a, b):
    return jnp.dot(a, b, preferred_element_type=F32)


def _layer_norm(y, g, b):
    mu = jnp.mean(y, axis=-1, keepdims=True)
    d = y - mu
    var = jnp.mean(d * d, axis=-1, keepdims=True)
    return d * lax.rsqrt(var + LN_EPS) * g + b


def _gelu_tanh(x):
    c = math.sqrt(2.0 / math.pi)
    return x * (0.5 * (1.0 + jnp.tanh(c * (x + 0.044715 * (x * x * x)))))


def _resident(shape):
    nd = len(shape)
    return pl.BlockSpec(shape, lambda i: (0,) * nd, pipeline_mode=pl.Buffered(1))


def _rows(tm, width):
    return pl.BlockSpec((tm, width), lambda i: (i, 0))


_HBM = pl.BlockSpec(memory_space=pl.ANY)


def _stage(cols):
    rows = 1 << ((STAGE_BYTES // (4 * cols)).bit_length() - 1)
    return [pltpu.VMEM((2, rows, cols), F32), pltpu.SemaphoreType.DMA((2,))]


def _fetch_cast(src, dst, stage, sem):
    total, cols = src.shape
    rows = min(stage.shape[1], total)
    assert total % rows == 0 and cols == stage.shape[2] and dst.shape == src.shape

    def copy(c):
        return pltpu.make_async_copy(src.at[pl.ds(c * rows, rows), :],
                                     stage.at[c % 2, pl.ds(0, rows), :], sem.at[c % 2])

    nchunk = total // rows
    copy(0).start()
    for c in range(nchunk):
        if c + 1 < nchunk:
            copy(c + 1).start()
        copy(c).wait()
        dst[pl.ds(c * rows, rows), :] = stage[c % 2, pl.ds(0, rows), :].astype(BF16)


def _params():
    return pltpu.CompilerParams(dimension_semantics=("arbitrary",),
                                vmem_limit_bytes=VMEM_LIMIT)


def _ffn_load(li, hi, wg_hbm, wu_hbm, wd_hbm, wg_v, wu_v, wd_v, st_f, sem_f, st_d, sem_d):
    _fetch_cast(wg_hbm.at[li, hi], wg_v, st_f, sem_f)
    _fetch_cast(wu_hbm.at[li, hi], wu_v, st_f, sem_f)
    _fetch_cast(wd_hbm.at[li, hi], wd_v, st_d, sem_d)


def _ffn_body(alpha, x_ref, wg_v, wu_v, wd_v, g, b):
    x = x_ref[...]
    xb = x.astype(BF16)
    gate = _dot(xb, wg_v[...])
    up = _dot(xb, wu_v[...])
    h = (gate * jax.nn.sigmoid(gate) * up).astype(BF16)
    ff = _dot(h, wd_v[...])
    return _layer_norm(alpha * x + 0.5 * ff, g, b)


def _ffn_kernel(alpha, li, hi, x_ref, wg_hbm, wu_hbm, wd_hbm, lng_ref, lnb_ref, o_ref,
                wg_v, wu_v, wd_v, st_f, sem_f, st_d, sem_d):
    @pl.when(pl.program_id(0) == 0)
    def _load_weights():
        _ffn_load(li, hi, wg_hbm, wu_hbm, wd_hbm, wg_v, wu_v, wd_v, st_f, sem_f, st_d, sem_d)

    s = 2 * hi
    o_ref[...] = _ffn_body(alpha, x_ref, wg_v, wu_v, wd_v,
                           lng_ref[li, s:s + 1, :], lnb_ref[li, s:s + 1, :])


def _ffn_ple_kernel(alpha, li, hi, x_ref, wg_hbm, wu_hbm, wd_hbm, lng_ref, lnb_ref,
                    p_ref, wp_hbm, wpg_hbm, o_ref,
                    wg_v, wu_v, wd_v, wp_v, wpg_v, st_f, sem_f, st_d, sem_d):
    @pl.when(pl.program_id(0) == 0)
    def _load_weights():
        _ffn_load(li, hi, wg_hbm, wu_hbm, wd_hbm, wg_v, wu_v, wd_v, st_f, sem_f, st_d, sem_d)
        _fetch_cast(wp_hbm.at[li], wp_v, st_d, sem_d)
        _fetch_cast(wpg_hbm.at[li], wpg_v, st_d, sem_d)

    s = 2 * hi
    y = _ffn_body(alpha, x_ref, wg_v, wu_v, wd_v,
                  lng_ref[li, s:s + 1, :], lnb_ref[li, s:s + 1, :])
    emb = _dot(p_ref[...].astype(BF16), wp_v[...])
    gate = jax.nn.sigmoid(_dot(y.astype(BF16), wpg_v[...]))
    o_ref[...] = y + emb * gate


def _ffn_call(x, wg, wu, wd, ln_g, ln_b, alpha, li, hi, tm, ple=None):
    n, d = x.shape
    f = wg.shape[-1]
    in_specs = [_rows(tm, d), _HBM, _HBM, _HBM, _resident(ln_g.shape), _resident(ln_b.shape)]
    args = [x, wg, wu, wd, ln_g, ln_b]
    scratch = [pltpu.VMEM((d, f), BF16), pltpu.VMEM((d, f), BF16), pltpu.VMEM((f, d), BF16)]
    if ple is None:
        body = functools.partial(_ffn_kernel, alpha, li, hi)
        name = "ffn_ln"
    else:
        p, wp, wpg = ple
        pd = p.shape[-1]
        in_specs += [pl.BlockSpec((None, tm, pd), lambda i: (li, i, 0)), _HBM, _HBM]
        args += [p, wp, wpg]
        scratch += [pltpu.VMEM((pd, d), BF16), pltpu.VMEM((d, d), BF16)]
        body = functools.partial(_ffn_ple_kernel, alpha, li, hi)
        name = "ffn_ln_ple"
    scratch += _stage(f) + _stage(d)
    return pl.pallas_call(
        body,
        grid=(n // tm,),
        in_specs=in_specs,
        out_specs=_rows(tm, d),
        out_shape=jax.ShapeDtypeStruct((n, d), F32),
        scratch_shapes=scratch,
        compiler_params=_params(),
        name=name,
    )(*args)


def _mixer_ab_kernel(alpha, li, ji, tm, tiles_per_seq,
                     sinks_ref, x_ref, pos_ref, win_hbm, inv_ref, sgn_ref,
                     sg_ref, sb_ref, ws_ref, bs_ref, wout_hbm, lng_ref, lnb_ref,
                     o_ref,
                     win_v, wout_v, st_i, sem_i, st_o, sem_o,
                     q_scr, kv_scr, u_scr, sv_scr, wcat_scr, cat_scr):
    i = pl.program_id(0)
    first_tile = (i % tiles_per_seq) == 0
    nblk = tm // WINDOW
    lane = lax.broadcasted_iota(jnp.int32, (1, LANES), 1)
    low_half = lane < HEAD_DIM

    @pl.when(i == 0)
    def _load_weights():
        _fetch_cast(win_hbm.at[ji], win_v, st_i, sem_i)
        _fetch_cast(wout_hbm.at[ji], wout_v, st_o, sem_o)
        r = lax.broadcasted_iota(jnp.int32, (CHUNK, CHUNK), 0)
        c = lax.broadcasted_iota(jnp.int32, (CHUNK, CHUNK), 1)
        tril = r >= c
        for j in range(SGU_GROUPS // 2):
            a = jnp.where(tril, ws_ref[ji, 2 * j], 0.0)
            bb = jnp.where(tril, ws_ref[ji, 2 * j + 1], 0.0)
            wcat_scr[j] = jnp.concatenate([a, bb], axis=1).astype(BF16)

    @pl.when(first_tile)
    def _reset_halo():
        kv_scr[0:WINDOW, :] = jnp.zeros((WINDOW, 4 * LANES), BF16)

    x = x_ref[...]
    h = _dot(x.astype(BF16), win_v[...])

    ang = pos_ref[...].astype(F32) * inv_ref[...]
    cos_t = jnp.cos(ang)
    sin_t = jnp.sin(ang) * sgn_ref[...]
    take_up = (lane % HEAD_DIM) < (ROT_DIM // 2)

    def rotary(t):
        up = pltpu.roll(t, LANES - ROT_DIM // 2, axis=1)
        dn = pltpu.roll(t, ROT_DIM // 2, axis=1)
        return t * cos_t + jnp.where(take_up, up, dn) * sin_t

    scale = HEAD_DIM ** -0.5
    for j in range(Q_W // LANES):
        qj = rotary(h[:, j * LANES:(j + 1) * LANES]) * scale
        q_scr[:, j * LANES:(j + 1) * LANES] = qj.astype(BF16)

    def dup_heads(t):
        sw = pltpu.roll(t, HEAD_DIM, axis=1)
        return jnp.where(low_half, t, sw), jnp.where(low_half, sw, t)

    k0, k1 = dup_heads(rotary(h[:, Q_W:Q_W + KV_W]))
    v0, v1 = dup_heads(h[:, Q_W + KV_W:Q_W + 2 * KV_W])
    kv_scr[WINDOW:, 0 * LANES:1 * LANES] = k0.astype(BF16)
    kv_scr[WINDOW:, 1 * LANES:2 * LANES] = k1.astype(BF16)
    kv_scr[WINDOW:, 2 * LANES:3 * LANES] = v0.astype(BF16)
    kv_scr[WINDOW:, 3 * LANES:4 * LANES] = v1.astype(BF16)

    su0 = Q_W + 2 * KV_W
    u_scr[...] = _gelu_tanh(h[:, su0:su0 + SGU_W])
    sv = _layer_norm(_gelu_tanh(h[:, su0 + SGU_W:su0 + 2 * SGU_W]),
                     sg_ref[ji:ji + 1, :], sb_ref[ji:ji + 1, :])
    sv_scr[...] = sv.astype(BF16)

    qi = lax.broadcasted_iota(jnp.int32, (2 * WINDOW, 2 * WINDOW), 0) % WINDOW
    kj = lax.broadcasted_iota(jnp.int32, (2 * WINDOW, 2 * WINDOW), 1)
    upper_ok = kj <= qi + WINDOW
    row_first = lax.broadcasted_iota(jnp.int32, (2 * WINDOW, 1), 0) < WINDOW

    def block(bi, carry):
        r0 = pl.multiple_of(bi * WINDOW, WINDOW)
        lower = jnp.where(jnp.logical_and(first_tile, bi == 0), WINDOW, qi + 1)
        valid = jnp.logical_and(upper_ok, kj >= lower)
        for kvh in range(ATT_KV_HEADS):
            kband = kv_scr[pl.ds(r0, 2 * WINDOW), kvh * LANES:(kvh + 1) * LANES]
            vband = kv_scr[pl.ds(r0, 2 * WINDOW), (2 + kvh) * LANES:(3 + kvh) * LANES]
            for pr in range(2):
                slab = kvh * 2 + pr
                qp = q_scr[pl.ds(r0, WINDOW), slab * LANES:(slab + 1) * LANES]
                zero = jnp.zeros_like(qp)
                qs = jnp.concatenate([jnp.where(low_half, qp, zero),
                                      jnp.where(low_half, zero, qp)], axis=0)
                s = lax.dot_general(qs, kband, (((1,), (1,)), ((), ())),
                                    preferred_element_type=F32)
                s = jnp.where(valid, s, NEG_INF)
                sink = jnp.where(row_first, sinks_ref[ji, 2 * slab], sinks_ref[ji, 2 * slab + 1])
                m = jnp.maximum(jnp.max(s, axis=-1, keepdims=True), sink)
                p = jnp.exp(s - m)
                denom = jnp.sum(p, axis=-1, keepdims=True) + jnp.exp(sink - m)
                o = _dot(p.astype(BF16), vband) / denom
                att = jnp.where(low_half, o[:WINDOW], o[WINDOW:])
                cat_scr[pl.ds(r0, WINDOW), slab * LANES:(slab + 1) * LANES] = att.astype(BF16)
        for j in range(SGU_GROUPS // 2):
            vp = sv_scr[pl.ds(r0, CHUNK), j * LANES:(j + 1) * LANES]
            zero = jnp.zeros_like(vp)
            rhs = jnp.concatenate([jnp.where(low_half, vp, zero),
                                   jnp.where(low_half, zero, vp)], axis=0)
            mixed = _dot(wcat_scr[j], rhs) + bs_ref[:, j * LANES:(j + 1) * LANES]
            out = u_scr[pl.ds(r0, CHUNK), j * LANES:(j + 1) * LANES] * mixed
            cat_scr[pl.ds(r0, CHUNK), Q_W + j * LANES:Q_W + (j + 1) * LANES] = out.astype(BF16)
        return carry

    lax.fori_loop(0, nblk, block, 0)

    kv_scr[0:WINDOW, :] = kv_scr[tm:tm + WINDOW, :]

    mix = _dot(cat_scr[...], wout_v[...])
    o_ref[...] = _layer_norm(alpha * x + mix, lng_ref[li, 1:2, :], lnb_ref[li, 1:2, :])


def _mixer_ab_call(x, pos, w_in, sinks, inv_lane, sgn_lane, sgu_g, sgu_b, w_s, bias_t,
                   w_out, ln_g, ln_b, alpha, li, ji, tm, seq):
    n, d = x.shape
    ab_in = w_in.shape[-1]
    body = functools.partial(_mixer_ab_kernel, alpha, li, ji, tm, seq // tm)
    in_specs = [
        pl.BlockSpec(memory_space=pltpu.SMEM),
        _rows(tm, d),
        _rows(tm, 1),
        _HBM,
        _resident((1, LANES)), _resident((1, LANES)),
        _resident(sgu_g.shape), _resident(sgu_b.shape),
        _resident(w_s.shape),
        _resident((CHUNK, SGU_W)),
        _HBM,
        _resident(ln_g.shape), _resident(ln_b.shape),
    ]
    scratch = [
        pltpu.VMEM((d, ab_in), BF16), pltpu.VMEM((Q_W + SGU_W, d), BF16),
        *_stage(ab_in), *_stage(d),
        pltpu.VMEM((tm, Q_W), BF16),
        pltpu.VMEM((tm + WINDOW, 4 * LANES), BF16),
        pltpu.VMEM((tm, SGU_W), F32),
        pltpu.VMEM((tm, SGU_W), BF16),
        pltpu.VMEM((SGU_GROUPS // 2, CHUNK, 2 * CHUNK), BF16),
        pltpu.VMEM((tm, Q_W + SGU_W), BF16),
    ]
    return pl.pallas_call(
        body,
        grid=(n // tm,),
        in_specs=in_specs,
        out_specs=_rows(tm, d),
        out_shape=jax.ShapeDtypeStruct((n, d), F32),
        scratch_shapes=scratch,
        compiler_params=_params(),
        name="mixer_attn_sgu",
    )(sinks, x, pos, w_in, inv_lane, sgn_lane, sgu_g, sgu_b, w_s, bias_t, w_out, ln_g, ln_b)


def _mixer_conv_kernel(alpha, li, ji, tm, tiles_per_seq,
                       x_ref, win_hbm, cw_ref, wout_hbm, lng_ref, lnb_ref, o_ref,
                       win_v, wout_v, st_i, sem_i, st_o, sem_o, cz_scr):
    i = pl.program_id(0)
    d = x_ref.shape[1]
    pad = SUBLANES

    @pl.when(i == 0)
    def _load_weights():
        _fetch_cast(win_hbm.at[ji], win_v, st_i, sem_i)
        _fetch_cast(wout_hbm.at[ji], wout_v, st_o, sem_o)

    @pl.when((i % tiles_per_seq) == 0)
    def _reset_halo():
        cz_scr[0:pad, :] = jnp.zeros((pad, d), F32)

    x = x_ref[...]
    h = _dot(x.astype(BF16), win_v[...])
    cz = h[:, d:2 * d] * h[:, 2 * d:3 * d]
    cz_scr[pad:, :] = cz
    y = cw_ref[ji, CONV_WIDTH - 1:CONV_WIDTH, :] * cz
    for t in range(CONV_WIDTH - 1):
        back = CONV_WIDTH - 1 - t
        y = y + cw_ref[ji, t:t + 1, :] * cz_scr[pad - back:pad - back + tm, :]
    cz_scr[0:pad, :] = cz_scr[tm:tm + pad, :]
    mix = _dot((h[:, 0:d] * y).astype(BF16), wout_v[...])
    o_ref[...] = _layer_norm(alpha * x + mix, lng_ref[li, 1:2, :], lnb_ref[li, 1:2, :])


def _mixer_conv_call(x, w_in, conv_w, w_out, ln_g, ln_b, alpha, li, ji, tm, seq):
    n, d = x.shape
    body = functools.partial(_mixer_conv_kernel, alpha, li, ji, tm, seq // tm)
    return pl.pallas_call(
        body,
        grid=(n // tm,),
        in_specs=[_rows(tm, d), _HBM, _resident(conv_w.shape), _HBM,
                  _resident(ln_g.shape), _resident(ln_b.shape)],
        out_specs=_rows(tm, d),
        out_shape=jax.ShapeDtypeStruct((n, d), F32),
        scratch_shapes=[pltpu.VMEM((d, 3 * d), BF16), pltpu.VMEM((d, d), BF16),
                        *_stage(3 * d), *_stage(d),
                        pltpu.VMEM((tm + SUBLANES, d), F32)],
        compiler_params=_params(),
        name="mixer_conv",
    )(x, w_in, conv_w, w_out, ln_g, ln_b)


def kernel(x, p, positions, ln_g, ln_b, ffn_w_gate, ffn_w_up, ffn_w_down, ab_w_in, ab_sinks,
           sgu_ln_g, sgu_ln_b, sgu_w_s, sgu_b_s, ab_w_out, sc_w_in, sc_conv_w, sc_w_out,
           ple_w_proj, ple_w_gate):
    bsz, seq, d = x.shape
    depth = p.shape[0]
    alpha = (2 * depth) ** 0.25
    tm = 512
    assert seq % tm == 0 and tm % WINDOW == 0
    assert sc_conv_w.shape[1] == CONV_WIDTH

    n = bsz * seq
    xs = x.reshape(n, d)
    pos = positions.reshape(n, 1)
    ps = p.reshape(depth, n, p.shape[-1])

    half = ROT_DIM // 2
    inv = jnp.power(ROPE_THETA, -jnp.arange(half, dtype=F32) * (2.0 / ROT_DIM))
    r = jnp.arange(LANES) % HEAD_DIM
    inv_lane = jnp.where(r < ROT_DIM, inv[r % half], 0.0).reshape(1, LANES).astype(F32)
    sgn_lane = jnp.where(r < half, -1.0, jnp.where(r < ROT_DIM, 1.0, 0.0)).reshape(1, LANES).astype(F32)

    for i in range(depth):
        j = i // 2
        xs = _ffn_call(xs, ffn_w_gate, ffn_w_up, ffn_w_down, ln_g, ln_b, alpha, i, 0, tm)
        if i % 2 == 0:
            bias_t = jnp.repeat(jnp.transpose(sgu_b_s[j]), SGU_GROUP_DIM, axis=1)
            xs = _mixer_ab_call(xs, pos, ab_w_in, ab_sinks, inv_lane, sgn_lane,
                                sgu_ln_g, sgu_ln_b, sgu_w_s, bias_t, ab_w_out, ln_g, ln_b,
                                alpha, i, j, tm, seq)
        else:
            xs = _mixer_conv_call(xs, sc_w_in, sc_conv_w, sc_w_out, ln_g, ln_b, alpha, i, j, tm, seq)
        xs = _ffn_call(xs, ffn_w_gate, ffn_w_up, ffn_w_down, ln_g, ln_b, alpha, i, 1, tm,
                       ple=(ps, ple_w_proj, ple_w_gate))
    return xs.reshape(bsz, seq, d)
```

```python
import functools
import math

import jax
import jax.numpy as jnp
from jax import lax
from jax.experimental import pallas as pl
from jax.experimental.pallas import tpu as pltpu

ATT_HEADS = 8
ATT_KV_HEADS = 2
HEAD_DIM = 64
WINDOW = 128
ROT_DIM = HEAD_DIM // 4
ROPE_THETA = 500000.0
SGU_GROUPS = 8
SGU_GROUP_DIM = 64
CHUNK = 128
CONV_WIDTH = 3
LN_EPS = 1e-5
NEG_INF = -1e30

Q_W = ATT_HEADS * HEAD_DIM
KV_W = ATT_KV_HEADS * HEAD_DIM
SGU_W = SGU_GROUPS * SGU_GROUP_DIM

LANES = 128
SUBLANES = 8
VMEM_LIMIT = 56 * 1024 * 1024
STAGE_BYTES = 3 * 512 * 1024
STAGE_SLOTS = 4
LN_ROWS = 8
LN_AHEAD = 6

BF16 = jnp.bfloat16
F32 = jnp.float32


def _dot(a, b):
    return jnp.dot(a, b, preferred_element_type=F32)


def _layer_norm(y, g, b):
    mu = jnp.mean(y, axis=-1, keepdims=True)
    d = y - mu
    var = jnp.mean(d * d, axis=-1, keepdims=True)
    return d * lax.rsqrt(var + LN_EPS) * g + b


def _gelu_tanh(x):
    c = math.sqrt(2.0 / math.pi)
    return x * (0.5 * (1.0 + jnp.tanh(c * (x + 0.044715 * (x * x * x)))))


def _resident(shape):
    nd = len(shape)
    return pl.BlockSpec(shape, lambda i: (0,) * nd, pipeline_mode=pl.Buffered(1))


def _rows(tm, width):
    return pl.BlockSpec((tm, width), lambda i: (i, 0))


_HBM = pl.BlockSpec(memory_space=pl.ANY)


def _stage(cols):
    rows = 1 << ((STAGE_BYTES // (4 * cols)).bit_length() - 1)
    return [pltpu.VMEM((STAGE_SLOTS, rows, cols), F32), pltpu.SemaphoreType.DMA((STAGE_SLOTS,))]


def _fetch_cast(jobs):
    chunks = []
    used = {}
    for src, dst, stage, sem in jobs:
        total, cols = src.shape
        rows = min(stage.shape[1], total)
        assert total % rows == 0 and cols == stage.shape[2] and dst.shape == src.shape
        for c in range(total // rows):
            slot = used.get(id(stage), 0) % STAGE_SLOTS
            used[id(stage)] = used.get(id(stage), 0) + 1
            staged = stage.at[slot, pl.ds(0, rows), :]
            copy = pltpu.make_async_copy(src.at[pl.ds(c * rows, rows), :], staged, sem.at[slot])
            chunks.append((copy, staged, dst.at[pl.ds(c * rows, rows), :]))
    for copy, _, _ in chunks[:STAGE_SLOTS]:
        copy.start()
    for n, (copy, staged, out) in enumerate(chunks):
        copy.wait()
        out[...] = staged[...].astype(BF16)
        if n + STAGE_SLOTS < len(chunks):
            chunks[n + STAGE_SLOTS][0].start()


def _params(flags=None):
    return pltpu.CompilerParams(dimension_semantics=("arbitrary",),
                                vmem_limit_bytes=VMEM_LIMIT, flags=flags)


def _ffn_jobs(li, hi, wg_hbm, wu_hbm, wd_hbm, wg_v, wu_v, wd_v, st_f, sem_f, st_d, sem_d):
    return [(wg_hbm.at[li, hi], wg_v, st_f, sem_f),
            (wu_hbm.at[li, hi], wu_v, st_f, sem_f),
            (wd_hbm.at[li, hi], wd_v, st_d, sem_d)]


def _fold_bits(y):
    bits = lax.bitcast_convert_type(y, jnp.int32)
    cols = bits[:, 0:LANES]
    for j in range(1, y.shape[1] // LANES):
        cols = cols | bits[:, j * LANES:(j + 1) * LANES]
    rows = cols[0:SUBLANES]
    for k in range(1, y.shape[0] // SUBLANES):
        rows = rows | cols[k * SUBLANES:(k + 1) * SUBLANES]
    return rows


def _zero_from(bits):
    top = jnp.max(bits, axis=(0, 1), keepdims=True)
    cleared = lax.shift_right_logical(lax.shift_right_logical(top, 16), 16)
    return lax.bitcast_convert_type(cleared, F32)


def _ffn_matmuls(x_ref, wg_v, wu_v, wd_v, x_keep, ff_keep, zero):
    x = x_ref[...]
    xb = x.astype(BF16)
    gate = _dot(xb, wg_v[...])
    up = _dot(xb, wu_v[...])
    one = 1.0 + zero
    h = (gate * (one / (one + jnp.exp(-gate))) * up).astype(BF16)
    x_keep[...] = x
    ff_keep[...] = _dot(h, wd_v[...])


def _ffn_finish(alpha, x_keep, ff_keep, g, b, store):
    ngroups = x_keep.shape[0] // LN_ROWS
    width = x_keep.shape[1]
    folded = []
    for c in range(ngroups):
        rows = pl.ds(c * LN_ROWS, LN_ROWS)
        a = alpha
        if c >= LN_AHEAD:
            z = lax.shift_right_logical(lax.shift_right_logical(folded[c - LN_AHEAD], 16), 16)
            a = alpha + jnp.tile(lax.bitcast_convert_type(z, F32),
                                 (LN_ROWS // SUBLANES, width // LANES))
        y = _layer_norm(a * x_keep[rows, :] + 0.5 * ff_keep[rows, :], g, b)
        store(rows, y)
        folded.append(_fold_bits(y))
    bits = folded[0]
    for f in folded[1:]:
        bits = bits | f
    return bits


def _staggered(load, matmuls, finish, after=lambda: None):
    i = pl.program_id(0)
    last = pl.num_programs(0) - 1

    @pl.when(i == 0)
    def _first():
        load()
        matmuls(0.0)

    @pl.when(jnp.logical_and(i > 0, i < last))
    def _steady():
        matmuls(_zero_from(finish()))
        after()

    @pl.when(i == last)
    def _last():
        finish()
        after()


def _ffn_kernel(alpha, li, hi, x_ref, wg_hbm, wu_hbm, wd_hbm, lng_ref, lnb_ref, o_ref,
                wg_v, wu_v, wd_v, st_f, sem_f, st_d, sem_d, x_keep, ff_keep):
    s = 2 * hi

    def load():
        _fetch_cast(_ffn_jobs(li, hi, wg_hbm, wu_hbm, wd_hbm, wg_v, wu_v, wd_v,
                              st_f, sem_f, st_d, sem_d))

    def store(rows, y):
        o_ref[rows, :] = y

    def finish():
        return _ffn_finish(alpha, x_keep, ff_keep,
                           lng_ref[li, s:s + 1, :], lnb_ref[li, s:s + 1, :], store)

    _staggered(load, functools.partial(_ffn_matmuls, x_ref, wg_v, wu_v, wd_v, x_keep, ff_keep),
               finish)


def _ffn_ple_kernel(alpha, li, hi, x_ref, wg_hbm, wu_hbm, wd_hbm, lng_ref, lnb_ref,
                    p_ref, wp_hbm, wpg_hbm, o_ref,
                    wg_v, wu_v, wd_v, wp_v, wpg_v, st_f, sem_f, st_d, sem_d, x_keep, ff_keep):
    s = 2 * hi

    def load():
        _fetch_cast(_ffn_jobs(li, hi, wg_hbm, wu_hbm, wd_hbm, wg_v, wu_v, wd_v,
                              st_f, sem_f, st_d, sem_d)
                    + [(wp_hbm.at[li], wp_v, st_d, sem_d), (wpg_hbm.at[li], wpg_v, st_d, sem_d)])

    def store(rows, y):
        o_ref[rows, :] = y

    def finish():
        return _ffn_finish(alpha, x_keep, ff_keep,
                           lng_ref[li, s:s + 1, :], lnb_ref[li, s:s + 1, :], store)

    def embed():
        y = o_ref[...]
        emb = _dot(p_ref[...].astype(BF16), wp_v[...])
        gate = jax.nn.sigmoid(_dot(y.astype(BF16), wpg_v[...]))
        o_ref[...] = y + emb * gate

    _staggered(load, functools.partial(_ffn_matmuls, x_ref, wg_v, wu_v, wd_v, x_keep, ff_keep),
               finish, embed)


def _ffn_call(x, wg, wu, wd, ln_g, ln_b, alpha, li, hi, tm, ple=None):
    n, d = x.shape
    f = wg.shape[-1]
    tiles = n // tm
    cur = lambda i: (jnp.minimum(i, tiles - 1), 0)
    prev = lambda i: (jnp.maximum(i - 1, 0), 0)
    in_specs = [pl.BlockSpec((tm, d), cur), _HBM, _HBM, _HBM,
                _resident(ln_g.shape), _resident(ln_b.shape)]
    args = [x, wg, wu, wd, ln_g, ln_b]
    scratch = [pltpu.VMEM((d, f), BF16), pltpu.VMEM((d, f), BF16), pltpu.VMEM((f, d), BF16)]
    if ple is None:
        body = functools.partial(_ffn_kernel, alpha, li, hi)
        name = "ffn_ln"
    else:
        p, wp, wpg = ple
        pd = p.shape[-1]
        in_specs += [pl.BlockSpec((None, tm, pd), lambda i: (li,) + prev(i)), _HBM, _HBM]
        args += [p, wp, wpg]
        scratch += [pltpu.VMEM((pd, d), BF16), pltpu.VMEM((d, d), BF16)]
        body = functools.partial(_ffn_ple_kernel, alpha, li, hi)
        name = "ffn_ln_ple"
    scratch += _stage(f) + _stage(d)
    scratch += [pltpu.VMEM((tm, d), F32), pltpu.VMEM((tm, d), F32)]
    return pl.pallas_call(
        body,
        grid=(tiles + 1,),
        in_specs=in_specs,
        out_specs=pl.BlockSpec((tm, d), prev),
        out_shape=jax.ShapeDtypeStruct((n, d), F32),
        scratch_shapes=scratch,
        compiler_params=_params(),
        name=name,
    )(*args)


def _mixer_ab_kernel(alpha, li, ji, tm, tiles_per_seq,
                     sinks_ref, x_ref, pos_ref, win_hbm, inv_ref, sgn_ref,
                     sg_ref, sb_ref, ws_ref, bs_ref, wout_hbm, lng_ref, lnb_ref,
                     o_ref,
                     win_v, wout_v, st_i, sem_i, st_o, sem_o,
                     q_scr, kv_scr, u_scr, sv_scr, wcat_scr, cat_scr):
    i = pl.program_id(0)
    first_tile = (i % tiles_per_seq) == 0
    nblk = tm // WINDOW
    lane = lax.broadcasted_iota(jnp.int32, (1, LANES), 1)
    low_half = lane < HEAD_DIM

    @pl.when(i == 0)
    def _load_weights():
        _fetch_cast([(win_hbm.at[ji], win_v, st_i, sem_i), (wout_hbm.at[ji], wout_v, st_o, sem_o)])
        r = lax.broadcasted_iota(jnp.int32, (CHUNK, CHUNK), 0)
        c = lax.broadcasted_iota(jnp.int32, (CHUNK, CHUNK), 1)
        tril = r >= c
        for j in range(SGU_GROUPS // 2):
            a = jnp.where(tril, ws_ref[ji, 2 * j], 0.0)
            bb = jnp.where(tril, ws_ref[ji, 2 * j + 1], 0.0)
            wcat_scr[j] = jnp.concatenate([a, bb], axis=1).astype(BF16)

    @pl.when(first_tile)
    def _reset_halo():
        kv_scr[0:WINDOW, :] = jnp.zeros((WINDOW, 4 * LANES), BF16)

    x = x_ref[...]
    h = _dot(x.astype(BF16), win_v[...])

    ang = pos_ref[...].astype(F32) * inv_ref[...]
    cos_t = jnp.cos(ang)
    sin_t = jnp.sin(ang) * sgn_ref[...]
    take_up = (lane % HEAD_DIM) < (ROT_DIM // 2)

    def rotary(t):
        up = pltpu.roll(t, LANES - ROT_DIM // 2, axis=1)
        dn = pltpu.roll(t, ROT_DIM // 2, axis=1)
        return t * cos_t + jnp.where(take_up, up, dn) * sin_t

    scale = HEAD_DIM ** -0.5
    for j in range(Q_W // LANES):
        qj = rotary(h[:, j * LANES:(j + 1) * LANES]) * scale
        q_scr[:, j * LANES:(j + 1) * LANES] = qj.astype(BF16)

    def dup_heads(t):
        sw = pltpu.roll(t, HEAD_DIM, axis=1)
        return jnp.where(low_half, t, sw), jnp.where(low_half, sw, t)

    k0, k1 = dup_heads(rotary(h[:, Q_W:Q_W + KV_W]))
    v0, v1 = dup_heads(h[:, Q_W + KV_W:Q_W + 2 * KV_W])
    kv_scr[WINDOW:, 0 * LANES:1 * LANES] = k0.astype(BF16)
    kv_scr[WINDOW:, 1 * LANES:2 * LANES] = k1.astype(BF16)
    kv_scr[WINDOW:, 2 * LANES:3 * LANES] = v0.astype(BF16)
    kv_scr[WINDOW:, 3 * LANES:4 * LANES] = v1.astype(BF16)

    su0 = Q_W + 2 * KV_W
    u_scr[...] = _gelu_tanh(h[:, su0:su0 + SGU_W])
    sv = _layer_norm(_gelu_tanh(h[:, su0 + SGU_W:su0 + 2 * SGU_W]),
                     sg_ref[ji:ji + 1, :], sb_ref[ji:ji + 1, :])
    sv_scr[...] = sv.astype(BF16)

    qi = lax.broadcasted_iota(jnp.int32, (2 * WINDOW, 2 * WINDOW), 0) % WINDOW
    kj = lax.broadcasted_iota(jnp.int32, (2 * WINDOW, 2 * WINDOW), 1)
    upper_ok = kj <= qi + WINDOW
    row_first = lax.broadcasted_iota(jnp.int32, (2 * WINDOW, 1), 0) < WINDOW

    def block(bi, carry):
        r0 = pl.multiple_of(bi * WINDOW, WINDOW)
        lower = jnp.where(jnp.logical_and(first_tile, bi == 0), WINDOW, qi + 1)
        valid = jnp.logical_and(upper_ok, kj >= lower)
        for kvh in range(ATT_KV_HEADS):
            kband = kv_scr[pl.ds(r0, 2 * WINDOW), kvh * LANES:(kvh + 1) * LANES]
            vband = kv_scr[pl.ds(r0, 2 * WINDOW), (2 + kvh) * LANES:(3 + kvh) * LANES]
            for pr in range(2):
                slab = kvh * 2 + pr
                qp = q_scr[pl.ds(r0, WINDOW), slab * LANES:(slab + 1) * LANES]
                zero = jnp.zeros_like(qp)
                qs = jnp.concatenate([jnp.where(low_half, qp, zero),
                                      jnp.where(low_half, zero, qp)], axis=0)
                s = lax.dot_general(qs, kband, (((1,), (1,)), ((), ())),
                                    preferred_element_type=F32)
                s = jnp.where(valid, s, NEG_INF)
                sink = jnp.where(row_first, sinks_ref[ji, 2 * slab], sinks_ref[ji, 2 * slab + 1])
                m = jnp.maximum(jnp.max(s, axis=-1, keepdims=True), sink)
                p = jnp.exp(s - m)
                denom = jnp.sum(p, axis=-1, keepdims=True) + jnp.exp(sink - m)
                o = _dot(p.astype(BF16), vband) / denom
                att = jnp.where(low_half, o[:WINDOW], o[WINDOW:])
                cat_scr[pl.ds(r0, WINDOW), slab * LANES:(slab + 1) * LANES] = att.astype(BF16)
        for j in range(SGU_GROUPS // 2):
            vp = sv_scr[pl.ds(r0, CHUNK), j * LANES:(j + 1) * LANES]
            zero = jnp.zeros_like(vp)
            rhs = jnp.concatenate([jnp.where(low_half, vp, zero),
                                   jnp.where(low_half, zero, vp)], axis=0)
            mixed = _dot(wcat_scr[j], rhs) + bs_ref[:, j * LANES:(j + 1) * LANES]
            out = u_scr[pl.ds(r0, CHUNK), j * LANES:(j + 1) * LANES] * mixed
            cat_scr[pl.ds(r0, CHUNK), Q_W + j * LANES:Q_W + (j + 1) * LANES] = out.astype(BF16)
        return carry

    lax.fori_loop(0, nblk, block, 0)

    kv_scr[0:WINDOW, :] = kv_scr[tm:tm + WINDOW, :]

    mix = _dot(cat_scr[...], wout_v[...])
    o_ref[...] = _layer_norm(alpha * x + mix, lng_ref[li, 1:2, :], lnb_ref[li, 1:2, :])


def _mixer_ab_call(x, pos, w_in, sinks, inv_lane, sgn_lane, sgu_g, sgu_b, w_s, bias_t,
                   w_out, ln_g, ln_b, alpha, li, ji, tm, seq):
    n, d = x.shape
    ab_in = w_in.shape[-1]
    body = functools.partial(_mixer_ab_kernel, alpha, li, ji, tm, seq // tm)
    in_specs = [
        pl.BlockSpec(memory_space=pltpu.SMEM),
        _rows(tm, d),
        _rows(tm, 1),
        _HBM,
        _resident((1, LANES)), _resident((1, LANES)),
        _resident(sgu_g.shape), _resident(sgu_b.shape),
        _resident(w_s.shape),
        _resident((CHUNK, SGU_W)),
        _HBM,
        _resident(ln_g.shape), _resident(ln_b.shape),
    ]
    scratch = [
        pltpu.VMEM((d, ab_in), BF16), pltpu.VMEM((Q_W + SGU_W, d), BF16),
        *_stage(ab_in), *_stage(d),
        pltpu.VMEM((tm, Q_W), BF16),
        pltpu.VMEM((tm + WINDOW, 4 * LANES), BF16),
        pltpu.VMEM((tm, SGU_W), F32),
        pltpu.VMEM((tm, SGU_W), BF16),
        pltpu.VMEM((SGU_GROUPS // 2, CHUNK, 2 * CHUNK), BF16),
        pltpu.VMEM((tm, Q_W + SGU_W), BF16),
    ]
    return pl.pallas_call(
        body,
        grid=(n // tm,),
        in_specs=in_specs,
        out_specs=_rows(tm, d),
        out_shape=jax.ShapeDtypeStruct((n, d), F32),
        scratch_shapes=scratch,
        compiler_params=_params(),
        name="mixer_attn_sgu",
    )(sinks, x, pos, w_in, inv_lane, sgn_lane, sgu_g, sgu_b, w_s, bias_t, w_out, ln_g, ln_b)


def _mixer_conv_kernel(alpha, li, ji, tm, tiles_per_seq,
                       x_ref, win_hbm, cw_ref, wout_hbm, lng_ref, lnb_ref, o_ref,
                       win_v, wout_v, st_i, sem_i, st_o, sem_o, cz_scr):
    i = pl.program_id(0)
    d = x_ref.shape[1]
    pad = SUBLANES

    @pl.when(i == 0)
    def _load_weights():
        _fetch_cast([(win_hbm.at[ji], win_v, st_i, sem_i), (wout_hbm.at[ji], wout_v, st_o, sem_o)])

    @pl.when((i % tiles_per_seq) == 0)
    def _reset_halo():
        cz_scr[0:pad, :] = jnp.zeros((pad, d), F32)

    x = x_ref[...]
    h = _dot(x.astype(BF16), win_v[...])
    cz = h[:, d:2 * d] * h[:, 2 * d:3 * d]
    cz_scr[pad:, :] = cz
    y = cw_ref[ji, CONV_WIDTH - 1:CONV_WIDTH, :] * cz
    for t in range(CONV_WIDTH - 1):
        back = CONV_WIDTH - 1 - t
        y = y + cw_ref[ji, t:t + 1, :] * cz_scr[pad - back:pad - back + tm, :]
    cz_scr[0:pad, :] = cz_scr[tm:tm + pad, :]
    mix = _dot((h[:, 0:d] * y).astype(BF16), wout_v[...])
    o_ref[...] = _layer_norm(alpha * x + mix, lng_ref[li, 1:2, :], lnb_ref[li, 1:2, :])


def _mixer_conv_call(x, w_in, conv_w, w_out, ln_g, ln_b, alpha, li, ji, tm, seq):
    n, d = x.shape
    body = functools.partial(_mixer_conv_kernel, alpha, li, ji, tm, seq // tm)
    return pl.pallas_call(
        body,
        grid=(n // tm,),
        in_specs=[_rows(tm, d), _HBM, _resident(conv_w.shape), _HBM,
                  _resident(ln_g.shape), _resident(ln_b.shape)],
        out_specs=_rows(tm, d),
        out_shape=jax.ShapeDtypeStruct((n, d), F32),
        scratch_shapes=[pltpu.VMEM((d, 3 * d), BF16), pltpu.VMEM((d, d), BF16),
                        *_stage(3 * d), *_stage(d),
                        pltpu.VMEM((tm + SUBLANES, d), F32)],
        compiler_params=_params(),
        name="mixer_conv",
    )(x, w_in, conv_w, w_out, ln_g, ln_b)


def kernel(x, p, positions, ln_g, ln_b, ffn_w_gate, ffn_w_up, ffn_w_down, ab_w_in, ab_sinks,
           sgu_ln_g, sgu_ln_b, sgu_w_s, sgu_b_s, ab_w_out, sc_w_in, sc_conv_w, sc_w_out,
           ple_w_proj, ple_w_gate):
    bsz, seq, d = x.shape
    depth = p.shape[0]
    alpha = (2 * depth) ** 0.25
    tm = 512
    assert seq % tm == 0 and tm % WINDOW == 0
    assert sc_conv_w.shape[1] == CONV_WIDTH

    n = bsz * seq
    xs = x.reshape(n, d)
    pos = positions.reshape(n, 1)
    ps = p.reshape(depth, n, p.shape[-1])

    half = ROT_DIM // 2
    inv = jnp.power(ROPE_THETA, -jnp.arange(half, dtype=F32) * (2.0 / ROT_DIM))
    r = jnp.arange(LANES) % HEAD_DIM
    inv_lane = jnp.where(r < ROT_DIM, inv[r % half], 0.0).reshape(1, LANES).astype(F32)
    sgn_lane = jnp.where(r < half, -1.0, jnp.where(r < ROT_DIM, 1.0, 0.0)).reshape(1, LANES).astype(F32)

    for i in range(depth):
        j = i // 2
        xs = _ffn_call(xs, ffn_w_gate, ffn_w_up, ffn_w_down, ln_g, ln_b, alpha, i, 0, tm)
        if i % 2 == 0:
            bias_t = jnp.repeat(jnp.transpose(sgu_b_s[j]), SGU_GROUP_DIM, axis=1)
            xs = _mixer_ab_call(xs, pos, ab_w_in, ab_sinks, inv_lane, sgn_lane,
                                sgu_ln_g, sgu_ln_b, sgu_w_s, bias_t, ab_w_out, ln_g, ln_b,
                                alpha, i, j, tm, seq)
        else:
            xs = _mixer_conv_call(xs, sc_w_in, sc_conv_w, sc_w_out, ln_g, ln_b, alpha, i, j, tm, seq)
        xs = _ffn_call(xs, ffn_w_gate, ffn_w_up, ffn_w_down, ln_g, ln_b, alpha, i, 1, tm,
                       ple=(ps, ple_w_proj, ple_w_gate))
    return xs.reshape(bsz, seq, d)
```

```python
import functools
import math

import jax
import jax.numpy as jnp
from jax import lax
from jax.experimental import pallas as pl
from jax.experimental.pallas import tpu as pltpu

ATT_HEADS = 8
ATT_KV_HEADS = 2
HEAD_DIM = 64
WINDOW = 128
ROT_DIM = HEAD_DIM // 4
ROPE_THETA = 500000.0
SGU_GROUPS = 8
SGU_GROUP_DIM = 64
CHUNK = 128
CONV_WIDTH = 3
LN_EPS = 1e-5
NEG_INF = -1e30

Q_W = ATT_HEADS * HEAD_DIM
KV_W = ATT_KV_HEADS * HEAD_DIM
SGU_W = SGU_GROUPS * SGU_GROUP_DIM

LANES = 128
SUBLANES = 8
VMEM_LIMIT = 56 * 1024 * 1024
STAGE_BYTES = 3 * 512 * 1024
STAGE_SLOTS = 4
LN_ROWS = 8
LN_AHEAD = 6

BF16 = jnp.bfloat16
F32 = jnp.float32


def _dot(a, b):
    return jnp.dot(a, b, preferred_element_type=F32)


def _layer_norm(y, g, b):
    mu = jnp.mean(y, axis=-1, keepdims=True)
    d = y - mu
    var = jnp.mean(d * d, axis=-1, keepdims=True)
    return d * lax.rsqrt(var + LN_EPS) * g + b


def _gelu_tanh(x):
    c = math.sqrt(2.0 / math.pi)
    return x * (0.5 * (1.0 + jnp.tanh(c * (x + 0.044715 * (x * x * x)))))


def _resident(shape):
    nd = len(shape)
    return pl.BlockSpec(shape, lambda i: (0,) * nd, pipeline_mode=pl.Buffered(1))


def _rows(tm, width):
    return pl.BlockSpec((tm, width), lambda i: (i, 0))


_HBM = pl.BlockSpec(memory_space=pl.ANY)


def _stage(cols):
    rows = 1 << ((STAGE_BYTES // (4 * cols)).bit_length() - 1)
    return [pltpu.VMEM((STAGE_SLOTS, rows, cols), F32), pltpu.SemaphoreType.DMA((STAGE_SLOTS,))]


def _fetch_cast(jobs):
    chunks = []
    used = {}
    for src, dst, stage, sem in jobs:
        total, cols = src.shape
        rows = min(stage.shape[1], total)
        assert total % rows == 0 and cols == stage.shape[2] and dst.shape == src.shape
        for c in range(total // rows):
            slot = used.get(id(stage), 0) % STAGE_SLOTS
            used[id(stage)] = used.get(id(stage), 0) + 1
            staged = stage.at[slot, pl.ds(0, rows), :]
            copy = pltpu.make_async_copy(src.at[pl.ds(c * rows, rows), :], staged, sem.at[slot])
            chunks.append((copy, staged, dst.at[pl.ds(c * rows, rows), :]))
    for copy, _, _ in chunks[:STAGE_SLOTS]:
        copy.start()
    for n, (copy, staged, out) in enumerate(chunks):
        copy.wait()
        out[...] = staged[...].astype(BF16)
        if n + STAGE_SLOTS < len(chunks):
            chunks[n + STAGE_SLOTS][0].start()


def _params(flags=None):
    return pltpu.CompilerParams(dimension_semantics=("arbitrary",),
                                vmem_limit_bytes=VMEM_LIMIT, flags=flags)


def _ffn_jobs(li, hi, wg_hbm, wu_hbm, wd_hbm, wg_v, wu_v, wd_v, st_f, sem_f, st_d, sem_d):
    return [(wg_hbm.at[li, hi], wg_v, st_f, sem_f),
            (wu_hbm.at[li, hi], wu_v, st_f, sem_f),
            (wd_hbm.at[li, hi], wd_v, st_d, sem_d)]


def _fold_bits(y):
    bits = lax.bitcast_convert_type(y, jnp.int32)
    cols = bits[:, 0:LANES]
    for j in range(1, y.shape[1] // LANES):
        cols = cols | bits[:, j * LANES:(j + 1) * LANES]
    rows = cols[0:SUBLANES]
    for k in range(1, y.shape[0] // SUBLANES):
        rows = rows | cols[k * SUBLANES:(k + 1) * SUBLANES]
    return rows


def _zero_from(bits):
    top = jnp.max(bits, axis=(0, 1), keepdims=True)
    cleared = lax.shift_right_logical(lax.shift_right_logical(top, 16), 16)
    return lax.bitcast_convert_type(cleared, F32)


def _ffn_matmuls(x_ref, wg_v, wu_v, wd_v, x_keep, ff_keep, zero):
    x = x_ref[...]
    xb = x.astype(BF16)
    gate = _dot(xb, wg_v[...])
    up = _dot(xb, wu_v[...])
    one = 1.0 + zero
    h = (gate * (one / (one + jnp.exp(-gate))) * up).astype(BF16)
    x_keep[...] = x
    ff_keep[...] = _dot(h, wd_v[...])


def _residual_ln(alpha, beta, x_keep, ff_keep, g, b, store):
    ngroups = x_keep.shape[0] // LN_ROWS
    width = x_keep.shape[1]
    folded = []
    for c in range(ngroups):
        rows = pl.ds(c * LN_ROWS, LN_ROWS)
        a = alpha
        if c >= LN_AHEAD:
            z = lax.shift_right_logical(lax.shift_right_logical(folded[c - LN_AHEAD], 16), 16)
            a = alpha + jnp.tile(lax.bitcast_convert_type(z, F32),
                                 (LN_ROWS // SUBLANES, width // LANES))
        r = ff_keep[rows, :]
        y = _layer_norm(a * x_keep[rows, :] + (r if beta == 1.0 else beta * r), g, b)
        store(rows, y)
        folded.append(_fold_bits(y))
    bits = folded[0]
    for f in folded[1:]:
        bits = bits | f
    return bits


def _staggered_maps(tiles):
    return (lambda i: (jnp.minimum(i, tiles - 1), 0)), (lambda i: (jnp.maximum(i - 1, 0), 0))


def _staggered(load, matmuls, finish, after=lambda: None):
    i = pl.program_id(0)
    last = pl.num_programs(0) - 1

    @pl.when(i == 0)
    def _first():
        load()
        matmuls(0.0)

    @pl.when(jnp.logical_and(i > 0, i < last))
    def _steady():
        matmuls(_zero_from(finish()))
        after()

    @pl.when(i == last)
    def _last():
        finish()
        after()


def _ffn_kernel(alpha, li, hi, x_ref, wg_hbm, wu_hbm, wd_hbm, lng_ref, lnb_ref, o_ref,
                wg_v, wu_v, wd_v, st_f, sem_f, st_d, sem_d, x_keep, ff_keep):
    s = 2 * hi

    def load():
        _fetch_cast(_ffn_jobs(li, hi, wg_hbm, wu_hbm, wd_hbm, wg_v, wu_v, wd_v,
                              st_f, sem_f, st_d, sem_d))

    def store(rows, y):
        o_ref[rows, :] = y

    def finish():
        return _residual_ln(alpha, 0.5, x_keep, ff_keep,
                           lng_ref[li, s:s + 1, :], lnb_ref[li, s:s + 1, :], store)

    _staggered(load, functools.partial(_ffn_matmuls, x_ref, wg_v, wu_v, wd_v, x_keep, ff_keep),
               finish)


def _ffn_ple_kernel(alpha, li, hi, x_ref, wg_hbm, wu_hbm, wd_hbm, lng_ref, lnb_ref,
                    p_ref, wp_hbm, wpg_hbm, o_ref,
                    wg_v, wu_v, wd_v, wp_v, wpg_v, st_f, sem_f, st_d, sem_d, x_keep, ff_keep):
    s = 2 * hi

    def load():
        _fetch_cast(_ffn_jobs(li, hi, wg_hbm, wu_hbm, wd_hbm, wg_v, wu_v, wd_v,
                              st_f, sem_f, st_d, sem_d)
                    + [(wp_hbm.at[li], wp_v, st_d, sem_d), (wpg_hbm.at[li], wpg_v, st_d, sem_d)])

    def store(rows, y):
        o_ref[rows, :] = y

    def finish():
        return _residual_ln(alpha, 0.5, x_keep, ff_keep,
                           lng_ref[li, s:s + 1, :], lnb_ref[li, s:s + 1, :], store)

    def embed():
        y = o_ref[...]
        emb = _dot(p_ref[...].astype(BF16), wp_v[...])
        gate = jax.nn.sigmoid(_dot(y.astype(BF16), wpg_v[...]))
        o_ref[...] = y + emb * gate

    _staggered(load, functools.partial(_ffn_matmuls, x_ref, wg_v, wu_v, wd_v, x_keep, ff_keep),
               finish, embed)


def _ffn_call(x, wg, wu, wd, ln_g, ln_b, alpha, li, hi, tm, ple=None):
    n, d = x.shape
    f = wg.shape[-1]
    tiles = n // tm
    cur, prev = _staggered_maps(tiles)
    in_specs = [pl.BlockSpec((tm, d), cur), _HBM, _HBM, _HBM,
                _resident(ln_g.shape), _resident(ln_b.shape)]
    args = [x, wg, wu, wd, ln_g, ln_b]
    scratch = [pltpu.VMEM((d, f), BF16), pltpu.VMEM((d, f), BF16), pltpu.VMEM((f, d), BF16)]
    if ple is None:
        body = functools.partial(_ffn_kernel, alpha, li, hi)
        name = "ffn_ln"
    else:
        p, wp, wpg = ple
        pd = p.shape[-1]
        in_specs += [pl.BlockSpec((None, tm, pd), lambda i: (li,) + prev(i)), _HBM, _HBM]
        args += [p, wp, wpg]
        scratch += [pltpu.VMEM((pd, d), BF16), pltpu.VMEM((d, d), BF16)]
        body = functools.partial(_ffn_ple_kernel, alpha, li, hi)
        name = "ffn_ln_ple"
    scratch += _stage(f) + _stage(d)
    scratch += [pltpu.VMEM((tm, d), F32), pltpu.VMEM((tm, d), F32)]
    return pl.pallas_call(
        body,
        grid=(tiles + 1,),
        in_specs=in_specs,
        out_specs=pl.BlockSpec((tm, d), prev),
        out_shape=jax.ShapeDtypeStruct((n, d), F32),
        scratch_shapes=scratch,
        compiler_params=_params(),
        name=name,
    )(*args)


def _mixer_ab_kernel(alpha, li, ji, tm, tiles_per_seq,
                     sinks_ref, x_ref, pos_ref, win_hbm, inv_ref, sgn_ref,
                     sg_ref, sb_ref, ws_ref, bs_ref, wout_hbm, lng_ref, lnb_ref,
                     o_ref,
                     win_v, wout_v, st_i, sem_i, st_o, sem_o,
                     q_scr, kv_scr, u_scr, sv_scr, wcat_scr, cat_scr, x_keep, mix_keep):
    i = pl.program_id(0)
    first_tile = (i % tiles_per_seq) == 0
    nblk = tm // WINDOW
    lane = lax.broadcasted_iota(jnp.int32, (1, LANES), 1)
    low_half = lane < HEAD_DIM

    def load():
        _fetch_cast([(win_hbm.at[ji], win_v, st_i, sem_i), (wout_hbm.at[ji], wout_v, st_o, sem_o)])
        r = lax.broadcasted_iota(jnp.int32, (CHUNK, CHUNK), 0)
        c = lax.broadcasted_iota(jnp.int32, (CHUNK, CHUNK), 1)
        tril = r >= c
        for j in range(SGU_GROUPS // 2):
            a = jnp.where(tril, ws_ref[ji, 2 * j], 0.0)
            bb = jnp.where(tril, ws_ref[ji, 2 * j + 1], 0.0)
            wcat_scr[j] = jnp.concatenate([a, bb], axis=1).astype(BF16)

    @pl.when(first_tile)
    def _reset_halo():
        kv_scr[0:WINDOW, :] = jnp.zeros((WINDOW, 4 * LANES), BF16)

    def finish():
        def store(rows, y):
            o_ref[rows, :] = y
        return _residual_ln(alpha, 1.0, x_keep, mix_keep,
                            lng_ref[li, 1:2, :], lnb_ref[li, 1:2, :], store)

    _staggered(load, functools.partial(
        _mixer_ab_tile, ji, tm, first_tile, low_half, lane, sinks_ref, x_ref, pos_ref, win_v,
        inv_ref, sgn_ref, sg_ref, sb_ref, bs_ref, wout_v, q_scr, kv_scr, u_scr, sv_scr,
        wcat_scr, cat_scr, x_keep, mix_keep), finish)


def _mixer_ab_tile(ji, tm, first_tile, low_half, lane, sinks_ref, x_ref, pos_ref, win_v,
                   inv_ref, sgn_ref, sg_ref, sb_ref, bs_ref, wout_v, q_scr, kv_scr, u_scr, sv_scr,
                   wcat_scr, cat_scr, x_keep, mix_keep, zero):
    nblk = tm // WINDOW
    x = x_ref[...]
    x_keep[...] = x
    h = _dot(x.astype(BF16), win_v[...])

    ang = pos_ref[...].astype(F32) * inv_ref[...]
    cos_t = jnp.cos(ang)
    sin_t = jnp.sin(ang) * sgn_ref[...]
    take_up = (lane % HEAD_DIM) < (ROT_DIM // 2)

    def rotary(t):
        up = pltpu.roll(t, LANES - ROT_DIM // 2, axis=1)
        dn = pltpu.roll(t, ROT_DIM // 2, axis=1)
        return t * cos_t + jnp.where(take_up, up, dn) * sin_t

    scale = HEAD_DIM ** -0.5
    for j in range(Q_W // LANES):
        qj = rotary(h[:, j * LANES:(j + 1) * LANES]) * scale
        q_scr[:, j * LANES:(j + 1) * LANES] = qj.astype(BF16)

    def dup_heads(t):
        sw = pltpu.roll(t, HEAD_DIM, axis=1)
        return jnp.where(low_half, t, sw), jnp.where(low_half, sw, t)

    k0, k1 = dup_heads(rotary(h[:, Q_W:Q_W + KV_W]))
    v0, v1 = dup_heads(h[:, Q_W + KV_W:Q_W + 2 * KV_W])
    kv_scr[WINDOW:, 0 * LANES:1 * LANES] = k0.astype(BF16)
    kv_scr[WINDOW:, 1 * LANES:2 * LANES] = k1.astype(BF16)
    kv_scr[WINDOW:, 2 * LANES:3 * LANES] = v0.astype(BF16)
    kv_scr[WINDOW:, 3 * LANES:4 * LANES] = v1.astype(BF16)

    su0 = Q_W + 2 * KV_W
    u_scr[...] = _gelu_tanh(h[:, su0:su0 + SGU_W])
    sv = _layer_norm(_gelu_tanh(h[:, su0 + SGU_W:su0 + 2 * SGU_W]),
                     sg_ref[ji:ji + 1, :], sb_ref[ji:ji + 1, :])
    sv_scr[...] = sv.astype(BF16)

    qi = lax.broadcasted_iota(jnp.int32, (2 * WINDOW, 2 * WINDOW), 0) % WINDOW
    kj = lax.broadcasted_iota(jnp.int32, (2 * WINDOW, 2 * WINDOW), 1)
    upper_ok = kj <= qi + WINDOW
    row_first = lax.broadcasted_iota(jnp.int32, (2 * WINDOW, 1), 0) < WINDOW

    band_ok = jnp.logical_and(upper_ok, kj > qi)
    band_ok_first = jnp.logical_and(upper_ok, kj >= jnp.where(first_tile, WINDOW, qi + 1))

    def block(bi):
        r0 = bi * WINDOW
        valid = band_ok_first if bi == 0 else band_ok
        tie = zero if bi == nblk - 1 else 0.0
        for kvh in range(ATT_KV_HEADS):
            kband = kv_scr[pl.ds(r0, 2 * WINDOW), kvh * LANES:(kvh + 1) * LANES]
            vband = kv_scr[pl.ds(r0, 2 * WINDOW), (2 + kvh) * LANES:(3 + kvh) * LANES]
            for pr in range(2):
                slab = kvh * 2 + pr
                qp = q_scr[pl.ds(r0, WINDOW), slab * LANES:(slab + 1) * LANES]
                blank = jnp.zeros_like(qp)
                qs = jnp.concatenate([jnp.where(low_half, qp, blank),
                                      jnp.where(low_half, blank, qp)], axis=0)
                s = lax.dot_general(qs, kband, (((1,), (1,)), ((), ())),
                                    preferred_element_type=F32)
                s = jnp.where(valid, s, NEG_INF)
                sink = jnp.where(row_first, sinks_ref[ji, 2 * slab],
                                 sinks_ref[ji, 2 * slab + 1]) + tie
                m = jnp.maximum(jnp.max(s, axis=-1, keepdims=True), sink)
                p = jnp.exp(s - m)
                denom = jnp.sum(p, axis=-1, keepdims=True) + jnp.exp(sink - m)
                o = _dot(p.astype(BF16), vband) / denom
                att = jnp.where(low_half, o[:WINDOW], o[WINDOW:])
                cat_scr[pl.ds(r0, WINDOW), slab * LANES:(slab + 1) * LANES] = att.astype(BF16)
        for j in range(SGU_GROUPS // 2):
            vp = sv_scr[pl.ds(r0, CHUNK), j * LANES:(j + 1) * LANES]
            blank = jnp.zeros_like(vp)
            rhs = jnp.concatenate([jnp.where(low_half, vp, blank),
                                   jnp.where(low_half, blank, vp)], axis=0)
            mixed = _dot(wcat_scr[j], rhs) + bs_ref[:, j * LANES:(j + 1) * LANES]
            out = u_scr[pl.ds(r0, CHUNK), j * LANES:(j + 1) * LANES] * mixed
            cat_scr[pl.ds(r0, CHUNK), Q_W + j * LANES:Q_W + (j + 1) * LANES] = out.astype(BF16)

    for bi in range(nblk):
        block(bi)

    kv_scr[0:WINDOW, :] = kv_scr[tm:tm + WINDOW, :]

    mix_keep[...] = _dot(cat_scr[...], wout_v[...])


def _mixer_ab_call(x, pos, w_in, sinks, inv_lane, sgn_lane, sgu_g, sgu_b, w_s, bias_t,
                   w_out, ln_g, ln_b, alpha, li, ji, tm, seq):
    n, d = x.shape
    ab_in = w_in.shape[-1]
    body = functools.partial(_mixer_ab_kernel, alpha, li, ji, tm, seq // tm)
    cur, prev = _staggered_maps(n // tm)
    in_specs = [
        pl.BlockSpec(memory_space=pltpu.SMEM),
        pl.BlockSpec((tm, d), cur),
        pl.BlockSpec((tm, 1), cur),
        _HBM,
        _resident((1, LANES)), _resident((1, LANES)),
        _resident(sgu_g.shape), _resident(sgu_b.shape),
        _resident(w_s.shape),
        _resident((CHUNK, SGU_W)),
        _HBM,
        _resident(ln_g.shape), _resident(ln_b.shape),
    ]
    scratch = [
        pltpu.VMEM((d, ab_in), BF16), pltpu.VMEM((Q_W + SGU_W, d), BF16),
        *_stage(ab_in), *_stage(d),
        pltpu.VMEM((tm, Q_W), BF16),
        pltpu.VMEM((tm + WINDOW, 4 * LANES), BF16),
        pltpu.VMEM((tm, SGU_W), F32),
        pltpu.VMEM((tm, SGU_W), BF16),
        pltpu.VMEM((SGU_GROUPS // 2, CHUNK, 2 * CHUNK), BF16),
        pltpu.VMEM((tm, Q_W + SGU_W), BF16),
        pltpu.VMEM((tm, d), F32), pltpu.VMEM((tm, d), F32),
    ]
    return pl.pallas_call(
        body,
        grid=(n // tm + 1,),
        in_specs=in_specs,
        out_specs=pl.BlockSpec((tm, d), prev),
        out_shape=jax.ShapeDtypeStruct((n, d), F32),
        scratch_shapes=scratch,
        compiler_params=_params(),
        name="mixer_attn_sgu",
    )(sinks, x, pos, w_in, inv_lane, sgn_lane, sgu_g, sgu_b, w_s, bias_t, w_out, ln_g, ln_b)


def _mixer_conv_kernel(alpha, li, ji, tm, tiles_per_seq,
                       x_ref, win_hbm, cw_ref, wout_hbm, lng_ref, lnb_ref, o_ref,
                       win_v, wout_v, st_i, sem_i, st_o, sem_o, cz_scr, x_keep, mix_keep):
    i = pl.program_id(0)
    d = x_ref.shape[1]
    pad = SUBLANES

    def load():
        _fetch_cast([(win_hbm.at[ji], win_v, st_i, sem_i), (wout_hbm.at[ji], wout_v, st_o, sem_o)])

    @pl.when((i % tiles_per_seq) == 0)
    def _reset_halo():
        cz_scr[0:pad, :] = jnp.zeros((pad, d), F32)

    def matmuls(zero):
        x = x_ref[...]
        x_keep[...] = x
        h = _dot(x.astype(BF16), win_v[...])
        cz = h[:, d:2 * d] * h[:, 2 * d:3 * d]
        cz_scr[pad:, :] = cz
        y = (cw_ref[ji, CONV_WIDTH - 1:CONV_WIDTH, :] + zero) * cz
        for t in range(CONV_WIDTH - 1):
            back = CONV_WIDTH - 1 - t
            y = y + cw_ref[ji, t:t + 1, :] * cz_scr[pad - back:pad - back + tm, :]
        cz_scr[0:pad, :] = cz_scr[tm:tm + pad, :]
        mix_keep[...] = _dot((h[:, 0:d] * y).astype(BF16), wout_v[...])

    def finish():
        def store(rows, y):
            o_ref[rows, :] = y
        return _residual_ln(alpha, 1.0, x_keep, mix_keep,
                            lng_ref[li, 1:2, :], lnb_ref[li, 1:2, :], store)

    _staggered(load, matmuls, finish)


def _mixer_conv_call(x, w_in, conv_w, w_out, ln_g, ln_b, alpha, li, ji, tm, seq):
    n, d = x.shape
    body = functools.partial(_mixer_conv_kernel, alpha, li, ji, tm, seq // tm)
    cur, prev = _staggered_maps(n // tm)
    return pl.pallas_call(
        body,
        grid=(n // tm + 1,),
        in_specs=[pl.BlockSpec((tm, d), cur), _HBM, _resident(conv_w.shape), _HBM,
                  _resident(ln_g.shape), _resident(ln_b.shape)],
        out_specs=pl.BlockSpec((tm, d), prev),
        out_shape=jax.ShapeDtypeStruct((n, d), F32),
        scratch_shapes=[pltpu.VMEM((d, 3 * d), BF16), pltpu.VMEM((d, d), BF16),
                        *_stage(3 * d), *_stage(d),
                        pltpu.VMEM((tm + SUBLANES, d), F32),
                        pltpu.VMEM((tm, d), F32), pltpu.VMEM((tm, d), F32)],
        compiler_params=_params(),
        name="mixer_conv",
    )(x, w_in, conv_w, w_out, ln_g, ln_b)


def kernel(x, p, positions, ln_g, ln_b, ffn_w_gate, ffn_w_up, ffn_w_down, ab_w_in, ab_sinks,
           sgu_ln_g, sgu_ln_b, sgu_w_s, sgu_b_s, ab_w_out, sc_w_in, sc_conv_w, sc_w_out,
           ple_w_proj, ple_w_gate):
    bsz, seq, d = x.shape
    depth = p.shape[0]
    alpha = (2 * depth) ** 0.25
    tm = 512
    assert seq % tm == 0 and tm % WINDOW == 0
    assert sc_conv_w.shape[1] == CONV_WIDTH

    n = bsz * seq
    xs = x.reshape(n, d)
    pos = positions.reshape(n, 1)
    ps = p.reshape(depth, n, p.shape[-1])

    half = ROT_DIM // 2
    inv = jnp.power(ROPE_THETA, -jnp.arange(half, dtype=F32) * (2.0 / ROT_DIM))
    r = jnp.arange(LANES) % HEAD_DIM
    inv_lane = jnp.where(r < ROT_DIM, inv[r % half], 0.0).reshape(1, LANES).astype(F32)
    sgn_lane = jnp.where(r < half, -1.0, jnp.where(r < ROT_DIM, 1.0, 0.0)).reshape(1, LANES).astype(F32)

    for i in range(depth):
        j = i // 2
        xs = _ffn_call(xs, ffn_w_gate, ffn_w_up, ffn_w_down, ln_g, ln_b, alpha, i, 0, tm)
        if i % 2 == 0:
            bias_t = jnp.repeat(jnp.transpose(sgu_b_s[j]), SGU_GROUP_DIM, axis=1)
            xs = _mixer_ab_call(xs, pos, ab_w_in, ab_sinks, inv_lane, sgn_lane,
                                sgu_ln_g, sgu_ln_b, sgu_w_s, bias_t, ab_w_out, ln_g, ln_b,
                                alpha, i, j, tm, seq)
        else:
            xs = _mixer_conv_call(xs, sc_w_in, sc_conv_w, sc_w_out, ln_g, ln_b, alpha, i, j, tm, seq)
        xs = _ffn_call(xs, ffn_w_gate, ffn_w_up, ffn_w_down, ln_g, ln_b, alpha, i, 1, tm,
                       ple=(ps, ple_w_proj, ple_w_gate))
    return xs.reshape(bsz, seq, d)
```

```python
import functools
import math

import jax
import jax.numpy as jnp
from jax import lax
from jax.experimental import pallas as pl
from jax.experimental.pallas import tpu as pltpu

ATT_HEADS = 8
ATT_KV_HEADS = 2
HEAD_DIM = 64
WINDOW = 128
ROT_DIM = HEAD_DIM // 4
ROPE_THETA = 500000.0
SGU_GROUPS = 8
SGU_GROUP_DIM = 64
CHUNK = 128
CONV_WIDTH = 3
LN_EPS = 1e-5
NEG_INF = -1e30

Q_W = ATT_HEADS * HEAD_DIM
KV_W = ATT_KV_HEADS * HEAD_DIM
SGU_W = SGU_GROUPS * SGU_GROUP_DIM

LANES = 128
SUBLANES = 8
VMEM_LIMIT = 56 * 1024 * 1024
STAGE_BYTES = 3 * 512 * 1024
STAGE_SLOTS = 4
TM = 512
TM_CONV = 512
LN_ROWS = 8
LN_AHEAD = 6

BF16 = jnp.bfloat16
F32 = jnp.float32


def _dot(a, b):
    return jnp.dot(a, b, preferred_element_type=F32)


def _layer_norm(y, g, b):
    mu = jnp.mean(y, axis=-1, keepdims=True)
    d = y - mu
    var = jnp.mean(d * d, axis=-1, keepdims=True)
    return d * lax.rsqrt(var + LN_EPS) * g + b


def _gelu_tanh(x):
    c = math.sqrt(2.0 / math.pi)
    return x * (0.5 * (1.0 + jnp.tanh(c * (x + 0.044715 * (x * x * x)))))


def _resident(shape):
    nd = len(shape)
    return pl.BlockSpec(shape, lambda i: (0,) * nd, pipeline_mode=pl.Buffered(1))


def _rows(tm, width):
    return pl.BlockSpec((tm, width), lambda i: (i, 0))


_HBM = pl.BlockSpec(memory_space=pl.ANY)


def _stage(cols):
    rows = 1 << ((STAGE_BYTES // (4 * cols)).bit_length() - 1)
    return [pltpu.VMEM((STAGE_SLOTS, rows, cols), F32), pltpu.SemaphoreType.DMA((STAGE_SLOTS,))]


def _fetch_cast(jobs):
    chunks = []
    used = {}
    for src, dst, stage, sem in jobs:
        total, cols = src.shape
        rows = min(stage.shape[1], total)
        assert total % rows == 0 and cols == stage.shape[2] and dst.shape == src.shape
        for c in range(total // rows):
            slot = used.get(id(stage), 0) % STAGE_SLOTS
            used[id(stage)] = used.get(id(stage), 0) + 1
            staged = stage.at[slot, pl.ds(0, rows), :]
            copy = pltpu.make_async_copy(src.at[pl.ds(c * rows, rows), :], staged, sem.at[slot])
            chunks.append((copy, staged, dst.at[pl.ds(c * rows, rows), :]))
    for copy, _, _ in chunks[:STAGE_SLOTS]:
        copy.start()
    for n, (copy, staged, out) in enumerate(chunks):
        copy.wait()
        out[...] = staged[...].astype(BF16)
        if n + STAGE_SLOTS < len(chunks):
            chunks[n + STAGE_SLOTS][0].start()


def _params(flags=None):
    return pltpu.CompilerParams(dimension_semantics=("arbitrary",),
                                vmem_limit_bytes=VMEM_LIMIT, flags=flags)


def _ffn_jobs(li, hi, wg_hbm, wu_hbm, wd_hbm, wg_v, wu_v, wd_v, st_f, sem_f, st_d, sem_d):
    return [(wg_hbm.at[li, hi], wg_v, st_f, sem_f),
            (wu_hbm.at[li, hi], wu_v, st_f, sem_f),
            (wd_hbm.at[li, hi], wd_v, st_d, sem_d)]


def _fold_bits(y):
    bits = lax.bitcast_convert_type(y, jnp.int32)
    cols = bits[:, 0:LANES]
    for j in range(1, y.shape[1] // LANES):
        cols = cols | bits[:, j * LANES:(j + 1) * LANES]
    rows = cols[0:SUBLANES]
    for k in range(1, y.shape[0] // SUBLANES):
        rows = rows | cols[k * SUBLANES:(k + 1) * SUBLANES]
    return rows


def _zero_from(bits):
    top = jnp.max(bits, axis=(0, 1), keepdims=True)
    cleared = lax.shift_right_logical(lax.shift_right_logical(top, 16), 16)
    return lax.bitcast_convert_type(cleared, F32)


def _ffn_matmuls(x_ref, wg_v, wu_v, wd_v, x_keep, ff_keep, zero):
    x = x_ref[...]
    xb = x.astype(BF16)
    gate = _dot(xb, wg_v[...])
    up = _dot(xb, wu_v[...])
    one = 1.0 + zero
    h = (gate * (one / (one + jnp.exp(-gate))) * up).astype(BF16)
    x_keep[...] = x
    ff_keep[...] = _dot(h, wd_v[...])


def _residual_ln(alpha, beta, x_keep, ff_keep, g, b, store):
    ngroups = x_keep.shape[0] // LN_ROWS
    width = x_keep.shape[1]
    folded = []
    for c in range(ngroups):
        rows = pl.ds(c * LN_ROWS, LN_ROWS)
        a = alpha
        if c >= LN_AHEAD:
            z = lax.shift_right_logical(lax.shift_right_logical(folded[c - LN_AHEAD], 16), 16)
            a = alpha + jnp.tile(lax.bitcast_convert_type(z, F32),
                                 (LN_ROWS // SUBLANES, width // LANES))
        r = ff_keep[rows, :]
        y = _layer_norm(a * x_keep[rows, :] + (r if beta == 1.0 else beta * r), g, b)
        store(rows, y)
        folded.append(_fold_bits(y))
    bits = folded[0]
    for f in folded[1:]:
        bits = bits | f
    return bits


def _staggered_maps(tiles):
    return (lambda i: (jnp.minimum(i, tiles - 1), 0)), (lambda i: (jnp.maximum(i - 1, 0), 0))


def _staggered(load, matmuls, finish, after=lambda: None):
    i = pl.program_id(0)
    last = pl.num_programs(0) - 1

    @pl.when(i == 0)
    def _first():
        load()
        matmuls(0.0)

    @pl.when(jnp.logical_and(i > 0, i < last))
    def _steady():
        matmuls(_zero_from(finish()))
        after()

    @pl.when(i == last)
    def _last():
        finish()
        after()


def _ffn_kernel(alpha, li, hi, x_ref, wg_hbm, wu_hbm, wd_hbm, lng_ref, lnb_ref, o_ref,
                wg_v, wu_v, wd_v, st_f, sem_f, st_d, sem_d, x_keep, ff_keep):
    s = 2 * hi

    def load():
        _fetch_cast(_ffn_jobs(li, hi, wg_hbm, wu_hbm, wd_hbm, wg_v, wu_v, wd_v,
                              st_f, sem_f, st_d, sem_d))

    def store(rows, y):
        o_ref[rows, :] = y

    def finish():
        return _residual_ln(alpha, 0.5, x_keep, ff_keep,
                           lng_ref[li, s:s + 1, :], lnb_ref[li, s:s + 1, :], store)

    _staggered(load, functools.partial(_ffn_matmuls, x_ref, wg_v, wu_v, wd_v, x_keep, ff_keep),
               finish)


def _ffn_ple_kernel(alpha, li, hi, x_ref, wg_hbm, wu_hbm, wd_hbm, lng_ref, lnb_ref,
                    p_ref, wp_hbm, wpg_hbm, o_ref,
                    wg_v, wu_v, wd_v, wp_v, wpg_v, st_f, sem_f, st_d, sem_d, x_keep, ff_keep):
    s = 2 * hi

    def load():
        _fetch_cast(_ffn_jobs(li, hi, wg_hbm, wu_hbm, wd_hbm, wg_v, wu_v, wd_v,
                              st_f, sem_f, st_d, sem_d)
                    + [(wp_hbm.at[li], wp_v, st_d, sem_d), (wpg_hbm.at[li], wpg_v, st_d, sem_d)])

    def store(rows, y):
        o_ref[rows, :] = y

    def finish():
        return _residual_ln(alpha, 0.5, x_keep, ff_keep,
                           lng_ref[li, s:s + 1, :], lnb_ref[li, s:s + 1, :], store)

    def embed():
        y = o_ref[...]
        emb = _dot(p_ref[...].astype(BF16), wp_v[...])
        gate = jax.nn.sigmoid(_dot(y.astype(BF16), wpg_v[...]))
        o_ref[...] = y + emb * gate

    _staggered(load, functools.partial(_ffn_matmuls, x_ref, wg_v, wu_v, wd_v, x_keep, ff_keep),
               finish, embed)


def _ffn_call(x, wg, wu, wd, ln_g, ln_b, alpha, li, hi, tm, ple=None):
    n, d = x.shape
    f = wg.shape[-1]
    tiles = n // tm
    cur, prev = _staggered_maps(tiles)
    in_specs = [pl.BlockSpec((tm, d), cur), _HBM, _HBM, _HBM,
                _resident(ln_g.shape), _resident(ln_b.shape)]
    args = [x, wg, wu, wd, ln_g, ln_b]
    scratch = [pltpu.VMEM((d, f), BF16), pltpu.VMEM((d, f), BF16), pltpu.VMEM((f, d), BF16)]
    if ple is None:
        body = functools.partial(_ffn_kernel, alpha, li, hi)
        name = "ffn_ln"
    else:
        p, wp, wpg = ple
        pd = p.shape[-1]
        in_specs += [pl.BlockSpec((None, tm, pd), lambda i: (li,) + prev(i)), _HBM, _HBM]
        args += [p, wp, wpg]
        scratch += [pltpu.VMEM((pd, d), BF16), pltpu.VMEM((d, d), BF16)]
        body = functools.partial(_ffn_ple_kernel, alpha, li, hi)
        name = "ffn_ln_ple"
    scratch += _stage(f) + _stage(d)
    scratch += [pltpu.VMEM((tm, d), F32), pltpu.VMEM((tm, d), F32)]
    return pl.pallas_call(
        body,
        grid=(tiles + 1,),
        in_specs=in_specs,
        out_specs=pl.BlockSpec((tm, d), prev),
        out_shape=jax.ShapeDtypeStruct((n, d), F32),
        scratch_shapes=scratch,
        compiler_params=_params(),
        name=name,
    )(*args)


def _mixer_ab_kernel(alpha, li, ji, tm, tiles_per_seq,
                     sinks_ref, x_ref, xprev_ref, pos_ref, win_hbm, inv_ref, sgn_ref,
                     sg_ref, sb_ref, ws_ref, bs_ref, wout_hbm, lng_ref, lnb_ref,
                     o_ref,
                     win_v, wout_v, st_i, sem_i, st_o, sem_o,
                     q_scr, kv_scr, u_scr, sv_scr, wcat_scr, cat_scr):
    i = pl.program_id(0)
    last = pl.num_programs(0) - 1
    lane = lax.broadcasted_iota(jnp.int32, (1, LANES), 1)
    low_half = lane < HEAD_DIM
    slot = i % 2

    def load():
        _fetch_cast([(win_hbm.at[ji], win_v, st_i, sem_i), (wout_hbm.at[ji], wout_v, st_o, sem_o)])
        r = lax.broadcasted_iota(jnp.int32, (CHUNK, CHUNK), 0)
        c = lax.broadcasted_iota(jnp.int32, (CHUNK, CHUNK), 1)
        tril = r >= c
        for j in range(SGU_GROUPS // 2):
            a = jnp.where(tril, ws_ref[ji, 2 * j], 0.0)
            bb = jnp.where(tril, ws_ref[ji, 2 * j + 1], 0.0)
            wcat_scr[j] = jnp.concatenate([a, bb], axis=1).astype(BF16)

    def project():
        _mixer_ab_project(ji, tm, (i % tiles_per_seq) == 0, low_half, lane, x_ref, pos_ref,
                          win_v, inv_ref, sgn_ref, sg_ref, sb_ref,
                          q_scr.at[slot], kv_scr.at[slot], kv_scr.at[1 - slot],
                          u_scr.at[slot], sv_scr.at[slot])

    def attend():
        mix = _mixer_ab_attend(ji, tm, ((i - 1) % tiles_per_seq) == 0, low_half, sinks_ref,
                               bs_ref, wout_v, q_scr.at[1 - slot], kv_scr.at[1 - slot],
                               u_scr.at[1 - slot], sv_scr.at[1 - slot], wcat_scr, cat_scr)
        o_ref[...] = _layer_norm(alpha * xprev_ref[...] + mix,
                                 lng_ref[li, 1:2, :], lnb_ref[li, 1:2, :])

    @pl.when(i == 0)
    def _first():
        load()
        project()

    @pl.when(jnp.logical_and(i > 0, i < last))
    def _steady():
        attend()
        project()

    @pl.when(i == last)
    def _last():
        attend()


def _mixer_ab_project(ji, tm, first_tile, low_half, lane, x_ref, pos_ref, win_v,
                      inv_ref, sgn_ref, sg_ref, sb_ref, q_scr, kv_scr, kv_other, u_scr, sv_scr):
    x = x_ref[...]
    h = _dot(x.astype(BF16), win_v[...])

    nseg = LANES // ROT_DIM
    seg = tm // nseg
    pos = pos_ref[...].astype(F32)
    packed = jnp.zeros((seg, LANES), F32)
    for s in range(nseg):
        in_seg = (lane // ROT_DIM) == s
        packed = jnp.where(in_seg, pos[s * seg:(s + 1) * seg, :], packed)
    ang = packed * inv_ref[...]
    cos_p = jnp.cos(ang)
    sin_p = jnp.sin(ang) * sgn_ref[...]
    rot_lo = lane < ROT_DIM
    rot_hi = jnp.logical_and(lane >= HEAD_DIM, lane < HEAD_DIM + ROT_DIM)

    def unpack(t, fill):
        parts = []
        for s in range(nseg):
            lo = t if s == 0 else pltpu.roll(t, LANES - s * ROT_DIM, axis=1)
            shift = (HEAD_DIM - s * ROT_DIM) % LANES
            hi = t if shift == 0 else pltpu.roll(t, shift, axis=1)
            parts.append(jnp.where(rot_lo, lo, jnp.where(rot_hi, hi, fill)))
        return jnp.concatenate(parts, axis=0)

    cos_t = unpack(cos_p, 1.0)
    sin_t = unpack(sin_p, 0.0)
    take_up = (lane % HEAD_DIM) < (ROT_DIM // 2)

    def rotary(t):
        up = pltpu.roll(t, LANES - ROT_DIM // 2, axis=1)
        dn = pltpu.roll(t, ROT_DIM // 2, axis=1)
        return t * cos_t + jnp.where(take_up, up, dn) * sin_t

    scale = HEAD_DIM ** -0.5
    for j in range(Q_W // LANES):
        qj = rotary(h[:, j * LANES:(j + 1) * LANES]) * scale
        q_scr[:, j * LANES:(j + 1) * LANES] = qj.astype(BF16)

    def dup_heads(t):
        sw = pltpu.roll(t, HEAD_DIM, axis=1)
        return jnp.where(low_half, t, sw), jnp.where(low_half, sw, t)

    k0, k1 = dup_heads(rotary(h[:, Q_W:Q_W + KV_W]))
    v0, v1 = dup_heads(h[:, Q_W + KV_W:Q_W + 2 * KV_W])
    kv_scr[WINDOW:, 0 * LANES:1 * LANES] = k0.astype(BF16)
    kv_scr[WINDOW:, 1 * LANES:2 * LANES] = k1.astype(BF16)
    kv_scr[WINDOW:, 2 * LANES:3 * LANES] = v0.astype(BF16)
    kv_scr[WINDOW:, 3 * LANES:4 * LANES] = v1.astype(BF16)

    su0 = Q_W + 2 * KV_W
    u_scr[...] = _gelu_tanh(h[:, su0:su0 + SGU_W])
    sv = _layer_norm(_gelu_tanh(h[:, su0 + SGU_W:su0 + 2 * SGU_W]),
                     sg_ref[ji:ji + 1, :], sb_ref[ji:ji + 1, :])
    sv_scr[...] = sv.astype(BF16)

    halo = kv_other[tm:tm + WINDOW, :]
    kv_scr[0:WINDOW, :] = jnp.where(first_tile, jnp.zeros_like(halo), halo)


def _mixer_ab_attend(ji, tm, first_tile, low_half, sinks_ref, bs_ref, wout_v,
                     q_scr, kv_scr, u_scr, sv_scr, wcat_scr, cat_scr):
    nblk = tm // WINDOW
    qi = lax.broadcasted_iota(jnp.int32, (2 * WINDOW, 2 * WINDOW), 0) % WINDOW
    kj = lax.broadcasted_iota(jnp.int32, (2 * WINDOW, 2 * WINDOW), 1)
    upper_ok = kj <= qi + WINDOW
    row_first = lax.broadcasted_iota(jnp.int32, (2 * WINDOW, 1), 0) < WINDOW

    band_ok = jnp.logical_and(upper_ok, kj > qi)
    band_ok_first = jnp.logical_and(upper_ok, kj >= jnp.where(first_tile, WINDOW, qi + 1))

    def block(bi):
        r0 = bi * WINDOW
        valid = band_ok_first if bi == 0 else band_ok
        for kvh in range(ATT_KV_HEADS):
            kband = kv_scr[pl.ds(r0, 2 * WINDOW), kvh * LANES:(kvh + 1) * LANES]
            vband = kv_scr[pl.ds(r0, 2 * WINDOW), (2 + kvh) * LANES:(3 + kvh) * LANES]
            for pr in range(2):
                slab = kvh * 2 + pr
                qp = q_scr[pl.ds(r0, WINDOW), slab * LANES:(slab + 1) * LANES]
                blank = jnp.zeros_like(qp)
                qs = jnp.concatenate([jnp.where(low_half, qp, blank),
                                      jnp.where(low_half, blank, qp)], axis=0)
                s = lax.dot_general(qs, kband, (((1,), (1,)), ((), ())),
                                    preferred_element_type=F32)
                s = jnp.where(valid, s, NEG_INF)
                sink = jnp.where(row_first, sinks_ref[ji, 2 * slab], sinks_ref[ji, 2 * slab + 1])
                m = jnp.max(s, axis=-1, keepdims=True)
                p = jnp.exp(s - m)
                denom = jnp.sum(p, axis=-1, keepdims=True) + jnp.exp(sink - m)
                o = _dot(p.astype(BF16), vband) / denom
                att = jnp.where(low_half, o[:WINDOW], o[WINDOW:])
                cat_scr[pl.ds(r0, WINDOW), slab * LANES:(slab + 1) * LANES] = att.astype(BF16)
        for j in range(SGU_GROUPS // 2):
            vp = sv_scr[pl.ds(r0, CHUNK), j * LANES:(j + 1) * LANES]
            blank = jnp.zeros_like(vp)
            rhs = jnp.concatenate([jnp.where(low_half, vp, blank),
                                   jnp.where(low_half, blank, vp)], axis=0)
            mixed = _dot(wcat_scr[j], rhs) + bs_ref[:, j * LANES:(j + 1) * LANES]
            out = u_scr[pl.ds(r0, CHUNK), j * LANES:(j + 1) * LANES] * mixed
            cat_scr[pl.ds(r0, CHUNK), Q_W + j * LANES:Q_W + (j + 1) * LANES] = out.astype(BF16)

    for bi in range(nblk):
        block(bi)

    return _dot(cat_scr[...], wout_v[...])


def _mixer_ab_call(x, pos, w_in, sinks, inv_lane, sgn_lane, sgu_g, sgu_b, w_s, bias_t,
                   w_out, ln_g, ln_b, alpha, li, ji, tm, seq):
    n, d = x.shape
    ab_in = w_in.shape[-1]
    body = functools.partial(_mixer_ab_kernel, alpha, li, ji, tm, seq // tm)
    cur, prev = _staggered_maps(n // tm)
    in_specs = [
        pl.BlockSpec(memory_space=pltpu.SMEM),
        pl.BlockSpec((tm, d), cur),
        pl.BlockSpec((tm, d), prev),
        pl.BlockSpec((tm, 1), cur),
        _HBM,
        _resident((1, LANES)), _resident((1, LANES)),
        _resident(sgu_g.shape), _resident(sgu_b.shape),
        _resident(w_s.shape),
        _resident((CHUNK, SGU_W)),
        _HBM,
        _resident(ln_g.shape), _resident(ln_b.shape),
    ]
    scratch = [
        pltpu.VMEM((d, ab_in), BF16), pltpu.VMEM((Q_W + SGU_W, d), BF16),
        *_stage(ab_in), *_stage(d),
        pltpu.VMEM((2, tm, Q_W), BF16),
        pltpu.VMEM((2, tm + WINDOW, 4 * LANES), BF16),
        pltpu.VMEM((2, tm, SGU_W), F32),
        pltpu.VMEM((2, tm, SGU_W), BF16),
        pltpu.VMEM((SGU_GROUPS // 2, CHUNK, 2 * CHUNK), BF16),
        pltpu.VMEM((tm, Q_W + SGU_W), BF16),
    ]
    return pl.pallas_call(
        body,
        grid=(n // tm + 1,),
        in_specs=in_specs,
        out_specs=pl.BlockSpec((tm, d), prev),
        out_shape=jax.ShapeDtypeStruct((n, d), F32),
        scratch_shapes=scratch,
        compiler_params=_params(),
        name="mixer_attn_sgu",
    )(sinks, x, x, pos, w_in, inv_lane, sgn_lane, sgu_g, sgu_b, w_s, bias_t, w_out, ln_g, ln_b)


def _mixer_conv_kernel(alpha, li, ji, tm, tiles_per_seq,
                       x_ref, win_hbm, cw_ref, wout_hbm, lng_ref, lnb_ref, o_ref,
                       win_v, wout_v, st_i, sem_i, st_o, sem_o, cz_scr, x_keep, mix_keep):
    i = pl.program_id(0)
    d = x_ref.shape[1]
    pad = SUBLANES

    def load():
        _fetch_cast([(win_hbm.at[ji], win_v, st_i, sem_i), (wout_hbm.at[ji], wout_v, st_o, sem_o)])

    @pl.when((i % tiles_per_seq) == 0)
    def _reset_halo():
        cz_scr[0:pad, :] = jnp.zeros((pad, d), F32)

    def matmuls(zero):
        x = x_ref[...]
        x_keep[...] = x
        h = _dot(x.astype(BF16), win_v[...])
        cz = h[:, d:2 * d] * h[:, 2 * d:3 * d]
        cz_scr[pad:, :] = cz
        y = (cw_ref[ji, CONV_WIDTH - 1:CONV_WIDTH, :] + zero) * cz
        for t in range(CONV_WIDTH - 1):
            back = CONV_WIDTH - 1 - t
            y = y + cw_ref[ji, t:t + 1, :] * cz_scr[pad - back:pad - back + tm, :]
        cz_scr[0:pad, :] = cz_scr[tm:tm + pad, :]
        mix_keep[...] = _dot((h[:, 0:d] * y).astype(BF16), wout_v[...])

    def finish():
        def store(rows, y):
            o_ref[rows, :] = y
        return _residual_ln(alpha, 1.0, x_keep, mix_keep,
                            lng_ref[li, 1:2, :], lnb_ref[li, 1:2, :], store)

    _staggered(load, matmuls, finish)


def _mixer_conv_call(x, w_in, conv_w, w_out, ln_g, ln_b, alpha, li, ji, tm, seq):
    n, d = x.shape
    body = functools.partial(_mixer_conv_kernel, alpha, li, ji, tm, seq // tm)
    cur, prev = _staggered_maps(n // tm)
    return pl.pallas_call(
        body,
        grid=(n // tm + 1,),
        in_specs=[pl.BlockSpec((tm, d), cur), _HBM, _resident(conv_w.shape), _HBM,
                  _resident(ln_g.shape), _resident(ln_b.shape)],
        out_specs=pl.BlockSpec((tm, d), prev),
        out_shape=jax.ShapeDtypeStruct((n, d), F32),
        scratch_shapes=[pltpu.VMEM((d, 3 * d), BF16), pltpu.VMEM((d, d), BF16),
                        *_stage(3 * d), *_stage(d),
                        pltpu.VMEM((tm + SUBLANES, d), F32),
                        pltpu.VMEM((tm, d), F32), pltpu.VMEM((tm, d), F32)],
        compiler_params=_params(),
        name="mixer_conv",
    )(x, w_in, conv_w, w_out, ln_g, ln_b)


def kernel(x, p, positions, ln_g, ln_b, ffn_w_gate, ffn_w_up, ffn_w_down, ab_w_in, ab_sinks,
           sgu_ln_g, sgu_ln_b, sgu_w_s, sgu_b_s, ab_w_out, sc_w_in, sc_conv_w, sc_w_out,
           ple_w_proj, ple_w_gate):
    bsz, seq, d = x.shape
    depth = p.shape[0]
    alpha = (2 * depth) ** 0.25
    tm = TM
    assert seq % tm == 0 and tm % WINDOW == 0 and seq % TM_CONV == 0
    assert sc_conv_w.shape[1] == CONV_WIDTH

    n = bsz * seq
    xs = x.reshape(n, d)
    pos = positions.reshape(n, 1)
    ps = p.reshape(depth, n, p.shape[-1])

    half = ROT_DIM // 2
    inv = jnp.power(ROPE_THETA, -jnp.arange(half, dtype=F32) * (2.0 / ROT_DIM))
    r = jnp.arange(LANES) % ROT_DIM
    inv_lane = inv[r % half].reshape(1, LANES)
    sgn_lane = jnp.where(r < half, -1.0, 1.0).reshape(1, LANES).astype(F32)

    for i in range(depth):
        j = i // 2
        xs = _ffn_call(xs, ffn_w_gate, ffn_w_up, ffn_w_down, ln_g, ln_b, alpha, i, 0, tm)
        if i % 2 == 0:
            bias_t = jnp.repeat(jnp.transpose(sgu_b_s[j]), SGU_GROUP_DIM, axis=1)
            xs = _mixer_ab_call(xs, pos, ab_w_in, ab_sinks, inv_lane, sgn_lane,
                                sgu_ln_g, sgu_ln_b, sgu_w_s, bias_t, ab_w_out, ln_g, ln_b,
                                alpha, i, j, tm, seq)
        else:
            xs = _mixer_conv_call(xs, sc_w_in, sc_conv_w, sc_w_out, ln_g, ln_b, alpha, i, j,
                                  TM_CONV, seq)
        xs = _ffn_call(xs, ffn_w_gate, ffn_w_up, ffn_w_down, ln_g, ln_b, alpha, i, 1, tm,
                       ple=(ps, ple_w_proj, ple_w_gate))
    return xs.reshape(bsz, seq, d)
```

```python
import functools
import math

import jax
import jax.numpy as jnp
from jax import lax
from jax.experimental import pallas as pl
from jax.experimental.pallas import tpu as pltpu

ATT_HEADS = 8
ATT_KV_HEADS = 2
HEAD_DIM = 64
WINDOW = 128
ROT_DIM = HEAD_DIM // 4
ROPE_THETA = 500000.0
SGU_GROUPS = 8
SGU_GROUP_DIM = 64
CHUNK = 128
CONV_WIDTH = 3
LN_EPS = 1e-5
NEG_INF = -1e30

Q_W = ATT_HEADS * HEAD_DIM
KV_W = ATT_KV_HEADS * HEAD_DIM
SGU_W = SGU_GROUPS * SGU_GROUP_DIM

LANES = 128
SUBLANES = 8
BF16_ROWS = 16
VMEM_LIMIT = 56 * 1024 * 1024
STAGE_BYTES = 3 * 512 * 1024
STAGE_SLOTS = 4
TM = 512
LN_ROWS = 8
LN_AHEAD = 6

BF16 = jnp.bfloat16
F32 = jnp.float32


def _dot(a, b):
    return jnp.dot(a, b, preferred_element_type=F32)


def _layer_norm(y, g, b):
    mu = jnp.mean(y, axis=-1, keepdims=True)
    d = y - mu
    var = jnp.mean(d * d, axis=-1, keepdims=True)
    return d * lax.rsqrt(var + LN_EPS) * g + b


def _gelu_tanh(x):
    c = math.sqrt(2.0 / math.pi)
    return x * (0.5 * (1.0 + jnp.tanh(c * (x + 0.044715 * (x * x * x)))))


def _resident(shape):
    nd = len(shape)
    return pl.BlockSpec(shape, lambda i: (0,) * nd, pipeline_mode=pl.Buffered(1))


_HBM = pl.BlockSpec(memory_space=pl.ANY)


def _params():
    return pltpu.CompilerParams(dimension_semantics=("arbitrary",),
                                vmem_limit_bytes=VMEM_LIMIT)


def _stage(cols):
    rows = 1 << ((STAGE_BYTES // (4 * cols)).bit_length() - 1)
    return [pltpu.VMEM((STAGE_SLOTS, rows, cols), F32), pltpu.SemaphoreType.DMA((STAGE_SLOTS,))]


def _fetch_cast(jobs):
    chunks = []
    used = {}
    for src, dst, stage, sem in jobs:
        total, cols = src.shape
        rows = min(stage.shape[1], total)
        assert total % rows == 0 and cols == stage.shape[2] and dst.shape == src.shape
        for c in range(total // rows):
            slot = used.get(id(stage), 0) % STAGE_SLOTS
            used[id(stage)] = used.get(id(stage), 0) + 1
            staged = stage.at[slot, pl.ds(0, rows), :]
            copy = pltpu.make_async_copy(src.at[pl.ds(c * rows, rows), :], staged, sem.at[slot])
            chunks.append((copy, staged, dst.at[pl.ds(c * rows, rows), :]))
    for copy, _, _ in chunks[:STAGE_SLOTS]:
        copy.start()
    for n, (copy, staged, out) in enumerate(chunks):
        copy.wait()
        out[...] = staged[...].astype(BF16)
        if n + STAGE_SLOTS < len(chunks):
            chunks[n + STAGE_SLOTS][0].start()


class _Weights:
    def __init__(self, mats, given):
        self.mats, self.given = mats, given
        self.shapes = [tuple(a.shape[-2:]) for a, _ in mats]
        self.widths = sorted({s[1] for s in self.shapes})

    @property
    def args(self):
        return list(self.given) if self.given else [a for a, _ in self.mats]

    @property
    def in_specs(self):
        return [_resident(s) for s in self.shapes] if self.given else [_HBM] * len(self.mats)

    @property
    def scratch(self):
        if self.given:
            return []
        out = [pltpu.VMEM(s, BF16) for s in self.shapes]
        for w in self.widths:
            out += _stage(w)
        return out

    def bind(self, in_refs, scratch_refs):
        if self.given:
            return list(in_refs), (lambda: None)
        n = len(self.mats)
        stages = {w: (scratch_refs[n + 2 * k], scratch_refs[n + 2 * k + 1])
                  for k, w in enumerate(self.widths)}
        jobs = [(src.at[lead], dst) + stages[shape[1]]
                for src, (_, lead), dst, shape in zip(in_refs, self.mats, scratch_refs, self.shapes)]
        return list(scratch_refs[:n]), (lambda: _fetch_cast(jobs))


class _Casts:
    def __init__(self, mats, steps):
        self.mats = mats
        self.blocks = []
        for a, _ in mats:
            rows, cols = a.shape[-2:]
            rb = next(r for r in range(BF16_ROWS, rows + 1, BF16_ROWS)
                      if rows % r == 0 and rows // r <= steps)
            self.blocks.append((rb, rows // rb, rows, cols))

    @property
    def args(self):
        return [a for a, _ in self.mats]

    @property
    def in_specs(self):
        return [pl.BlockSpec((None,) * len(lead) + (rb, cols),
                             lambda i, lead=lead, nb=nb: lead + (jnp.minimum(i, nb - 1), 0))
                for (_, lead), (rb, nb, _, cols) in zip(self.mats, self.blocks)]

    @property
    def out_specs(self):
        return [pl.BlockSpec((rb, cols), lambda i, nb=nb: (jnp.minimum(i, nb - 1), 0))
                for rb, nb, _, cols in self.blocks]

    @property
    def out_shapes(self):
        return [jax.ShapeDtypeStruct((rows, cols), BF16) for _, _, rows, cols in self.blocks]

    @staticmethod
    def run(in_refs, out_refs):
        for src, dst in zip(in_refs, out_refs):
            dst[...] = src[...].astype(BF16)


def _split_refs(refs, n_main, weights, casts):
    nw, nc = len(weights.mats), len(casts.mats)
    main, rest = refs[:n_main], refs[n_main:]
    w_in, c_in = rest[:nw], rest[nw:nw + nc]
    o_ref, c_out = rest[nw + nc], rest[nw + nc + 1:nw + 2 * nc + 1]
    scratch = rest[nw + 2 * nc + 1:]
    ns = len(weights.scratch)
    w, load = weights.bind(w_in, scratch[:ns])
    _Casts.run(c_in, c_out)
    return main, w, load, o_ref, scratch[ns:]


def _call(body, name, tiles, tm, d, main_specs, main_args, weights, casts, scratch):
    _, prev = _staggered_maps(tiles)
    outs = pl.pallas_call(
        body,
        grid=(tiles + 1,),
        in_specs=main_specs + weights.in_specs + casts.in_specs,
        out_specs=[pl.BlockSpec((tm, d), prev)] + casts.out_specs,
        out_shape=[jax.ShapeDtypeStruct((tiles * tm, d), F32)] + casts.out_shapes,
        scratch_shapes=weights.scratch + scratch,
        compiler_params=_params(),
        name=name,
    )(*main_args, *weights.args, *casts.args)
    return outs[0], list(outs[1:])


def _fold_bits(y):
    bits = lax.bitcast_convert_type(y, jnp.int32)
    cols = bits[:, 0:LANES]
    for j in range(1, y.shape[1] // LANES):
        cols = cols | bits[:, j * LANES:(j + 1) * LANES]
    rows = cols[0:SUBLANES]
    for k in range(1, y.shape[0] // SUBLANES):
        rows = rows | cols[k * SUBLANES:(k + 1) * SUBLANES]
    return rows


def _zero_from(bits):
    top = jnp.max(bits, axis=(0, 1), keepdims=True)
    cleared = lax.shift_right_logical(lax.shift_right_logical(top, 16), 16)
    return lax.bitcast_convert_type(cleared, F32)


def _residual_ln(alpha, beta, x_keep, r_keep, g, b, store, chained):
    if not chained:
        beta_r = r_keep[...] if beta == 1.0 else beta * r_keep[...]
        store(slice(None), _layer_norm(alpha * x_keep[...] + beta_r, g, b))
        return None
    ngroups = x_keep.shape[0] // LN_ROWS
    width = x_keep.shape[1]
    folded = []
    for c in range(ngroups):
        rows = pl.ds(c * LN_ROWS, LN_ROWS)
        a = alpha
        if c >= LN_AHEAD:
            z = lax.shift_right_logical(lax.shift_right_logical(folded[c - LN_AHEAD], 16), 16)
            a = alpha + jnp.tile(lax.bitcast_convert_type(z, F32),
                                 (LN_ROWS // SUBLANES, width // LANES))
        r = r_keep[rows, :]
        y = _layer_norm(a * x_keep[rows, :] + (r if beta == 1.0 else beta * r), g, b)
        store(rows, y)
        folded.append(_fold_bits(y))
    bits = folded[0]
    for f in folded[1:]:
        bits = bits | f
    return bits


def _staggered_maps(tiles):
    return (lambda i: (jnp.minimum(i, tiles - 1), 0)), (lambda i: (jnp.maximum(i - 1, 0), 0))


def _staggered(load, matmuls, finish, after=lambda: None):
    i = pl.program_id(0)
    last = pl.num_programs(0) - 1

    @pl.when(i == 0)
    def _first():
        load()
        matmuls(0.0)

    @pl.when(jnp.logical_and(i > 0, i < last))
    def _steady():
        matmuls(_zero_from(finish(True)))
        after()

    @pl.when(i == last)
    def _last():
        finish(False)
        after()


def _ffn_matmuls(x_ref, wg_v, wu_v, wd_v, x_keep, ff_keep, zero):
    x = x_ref[...]
    xb = x.astype(BF16)
    gate = _dot(xb, wg_v[...])
    up = _dot(xb, wu_v[...])
    one = 1.0 + zero
    h = (gate * (one / (one + jnp.exp(-gate))) * up).astype(BF16)
    x_keep[...] = x
    ff_keep[...] = _dot(h, wd_v[...])


def _ffn_kernel(alpha, li, hi, ple, weights, casts, *refs):
    main, w, load, o_ref, (x_keep, ff_keep) = _split_refs(refs, 4 if ple else 3, weights, casts)
    x_ref, lng_ref, lnb_ref = main[:3]
    s = 2 * hi

    def store(rows, y):
        o_ref[rows, :] = y

    def finish(chained):
        return _residual_ln(alpha, 0.5, x_keep, ff_keep,
                            lng_ref[li, s:s + 1, :], lnb_ref[li, s:s + 1, :], store, chained)

    def embed():
        y = o_ref[...]
        emb = _dot(main[3][...].astype(BF16), w[3][...])
        gate = jax.nn.sigmoid(_dot(y.astype(BF16), w[4][...]))
        o_ref[...] = y + emb * gate

    matmuls = functools.partial(_ffn_matmuls, x_ref, w[0], w[1], w[2], x_keep, ff_keep)
    if ple:
        _staggered(load, matmuls, finish, embed)
    else:
        _staggered(load, matmuls, finish)


def _ffn_call(x, ln_g, ln_b, alpha, li, hi, tm, weights, casts, p=None):
    n, d = x.shape
    tiles = n // tm
    cur, prev = _staggered_maps(tiles)
    main_specs = [pl.BlockSpec((tm, d), cur), _resident(ln_g.shape), _resident(ln_b.shape)]
    main_args = [x, ln_g, ln_b]
    if p is not None:
        main_specs.append(pl.BlockSpec((None, tm, p.shape[-1]), lambda i: (li,) + prev(i)))
        main_args.append(p)
    body = functools.partial(_ffn_kernel, alpha, li, hi, p is not None, weights, casts)
    scratch = [pltpu.VMEM((tm, d), F32), pltpu.VMEM((tm, d), F32)]
    return _call(body, "ffn_ln" if p is None else "ffn_ln_ple", tiles, tm, d,
                 main_specs, main_args, weights, casts, scratch)


def _mixer_ab_kernel(alpha, li, ji, tm, tiles_per_seq, weights, casts, *refs):
    main, (win_v, wout_v), load_weights, o_ref, scratch = _split_refs(refs, 12, weights, casts)
    (sinks_ref, x_ref, xprev_ref, pos_ref, inv_ref, sgn_ref, sg_ref, sb_ref, ws_ref, bs_ref,
     lng_ref, lnb_ref) = main
    q_scr, kv_scr, u_scr, sv_scr, wcat_scr, cat_scr = scratch
    i = pl.program_id(0)
    last = pl.num_programs(0) - 1
    lane = lax.broadcasted_iota(jnp.int32, (1, LANES), 1)
    low_half = lane < HEAD_DIM
    slot = i % 2

    def load():
        load_weights()
        r = lax.broadcasted_iota(jnp.int32, (CHUNK, CHUNK), 0)
        c = lax.broadcasted_iota(jnp.int32, (CHUNK, CHUNK), 1)
        tril = r >= c
        for j in range(SGU_GROUPS // 2):
            a = jnp.where(tril, ws_ref[ji, 2 * j], 0.0)
            bb = jnp.where(tril, ws_ref[ji, 2 * j + 1], 0.0)
            wcat_scr[j] = jnp.concatenate([a, bb], axis=1).astype(BF16)

    def project():
        _mixer_ab_project(ji, tm, (i % tiles_per_seq) == 0, low_half, lane, x_ref, pos_ref,
                          win_v, inv_ref, sgn_ref, sg_ref, sb_ref,
                          q_scr.at[slot], kv_scr.at[slot], kv_scr.at[1 - slot],
                          u_scr.at[slot], sv_scr.at[slot])

    def attend():
        mix = _mixer_ab_attend(ji, tm, ((i - 1) % tiles_per_seq) == 0, low_half, sinks_ref,
                               bs_ref, wout_v, q_scr.at[1 - slot], kv_scr.at[1 - slot],
                               u_scr.at[1 - slot], sv_scr.at[1 - slot], wcat_scr, cat_scr)
        o_ref[...] = _layer_norm(alpha * xprev_ref[...] + mix,
                                 lng_ref[li, 1:2, :], lnb_ref[li, 1:2, :])

    @pl.when(i == 0)
    def _first():
        load()
        project()

    @pl.when(jnp.logical_and(i > 0, i < last))
    def _steady():
        attend()
        project()

    @pl.when(i == last)
    def _last():
        attend()


def _mixer_ab_project(ji, tm, first_tile, low_half, lane, x_ref, pos_ref, win_v,
                      inv_ref, sgn_ref, sg_ref, sb_ref, q_scr, kv_scr, kv_other, u_scr, sv_scr):
    x = x_ref[...]
    h = _dot(x.astype(BF16), win_v[...])

    nseg = LANES // ROT_DIM
    seg = tm // nseg
    pos = pos_ref[...].astype(F32)
    packed = jnp.zeros((seg, LANES), F32)
    for s in range(nseg):
        in_seg = (lane // ROT_DIM) == s
        packed = jnp.where(in_seg, pos[s * seg:(s + 1) * seg, :], packed)
    ang = packed * inv_ref[...]
    cos_p = jnp.cos(ang)
    sin_p = jnp.sin(ang) * sgn_ref[...]
    rot_lo = lane < ROT_DIM
    rot_hi = jnp.logical_and(lane >= HEAD_DIM, lane < HEAD_DIM + ROT_DIM)

    def unpack(t, fill):
        parts = []
        for s in range(nseg):
            lo = t if s == 0 else pltpu.roll(t, LANES - s * ROT_DIM, axis=1)
            shift = (HEAD_DIM - s * ROT_DIM) % LANES
            hi = t if shift == 0 else pltpu.roll(t, shift, axis=1)
            parts.append(jnp.where(rot_lo, lo, jnp.where(rot_hi, hi, fill)))
        return jnp.concatenate(parts, axis=0)

    cos_t = unpack(cos_p, 1.0)
    sin_t = unpack(sin_p, 0.0)
    take_up = (lane % HEAD_DIM) < (ROT_DIM // 2)

    def rotary(t):
        up = pltpu.roll(t, LANES - ROT_DIM // 2, axis=1)
        dn = pltpu.roll(t, ROT_DIM // 2, axis=1)
        return t * cos_t + jnp.where(take_up, up, dn) * sin_t

    scale = HEAD_DIM ** -0.5
    for j in range(Q_W // LANES):
        qj = rotary(h[:, j * LANES:(j + 1) * LANES]) * scale
        q_scr[:, j * LANES:(j + 1) * LANES] = qj.astype(BF16)

    def dup_heads(t):
        sw = pltpu.roll(t, HEAD_DIM, axis=1)
        return jnp.where(low_half, t, sw), jnp.where(low_half, sw, t)

    k0, k1 = dup_heads(rotary(h[:, Q_W:Q_W + KV_W]))
    v0, v1 = dup_heads(h[:, Q_W + KV_W:Q_W + 2 * KV_W])
    kv_scr[WINDOW:, 0 * LANES:1 * LANES] = k0.astype(BF16)
    kv_scr[WINDOW:, 1 * LANES:2 * LANES] = k1.astype(BF16)
    kv_scr[WINDOW:, 2 * LANES:3 * LANES] = v0.astype(BF16)
    kv_scr[WINDOW:, 3 * LANES:4 * LANES] = v1.astype(BF16)

    su0 = Q_W + 2 * KV_W
    u_scr[...] = _gelu_tanh(h[:, su0:su0 + SGU_W])
    sv = _layer_norm(_gelu_tanh(h[:, su0 + SGU_W:su0 + 2 * SGU_W]),
                     sg_ref[ji:ji + 1, :], sb_ref[ji:ji + 1, :])
    sv_scr[...] = sv.astype(BF16)

    halo = kv_other[tm:tm + WINDOW, :]
    kv_scr[0:WINDOW, :] = jnp.where(first_tile, jnp.zeros_like(halo), halo)


def _mixer_ab_attend(ji, tm, first_tile, low_half, sinks_ref, bs_ref, wout_v,
                     q_scr, kv_scr, u_scr, sv_scr, wcat_scr, cat_scr):
    nblk = tm // WINDOW
    qi = lax.broadcasted_iota(jnp.int32, (2 * WINDOW, 2 * WINDOW), 0) % WINDOW
    kj = lax.broadcasted_iota(jnp.int32, (2 * WINDOW, 2 * WINDOW), 1)
    upper_ok = kj <= qi + WINDOW
    row_first = lax.broadcasted_iota(jnp.int32, (2 * WINDOW, 1), 0) < WINDOW

    band_ok = jnp.logical_and(upper_ok, kj > qi)
    band_ok_first = jnp.logical_and(upper_ok, kj >= jnp.where(first_tile, WINDOW, qi + 1))

    def block(bi):
        r0 = bi * WINDOW
        valid = band_ok_first if bi == 0 else band_ok
        for kvh in range(ATT_KV_HEADS):
            kband = kv_scr[pl.ds(r0, 2 * WINDOW), kvh * LANES:(kvh + 1) * LANES]
            vband = kv_scr[pl.ds(r0, 2 * WINDOW), (2 + kvh) * LANES:(3 + kvh) * LANES]
            for pr in range(2):
                slab = kvh * 2 + pr
                qp = q_scr[pl.ds(r0, WINDOW), slab * LANES:(slab + 1) * LANES]
                blank = jnp.zeros_like(qp)
                qs = jnp.concatenate([jnp.where(low_half, qp, blank),
                                      jnp.where(low_half, blank, qp)], axis=0)
                s = lax.dot_general(qs, kband, (((1,), (1,)), ((), ())),
                                    preferred_element_type=F32)
                s = jnp.where(valid, s, NEG_INF)
                sink = jnp.where(row_first, sinks_ref[ji, 2 * slab], sinks_ref[ji, 2 * slab + 1])
                m = jnp.max(s, axis=-1, keepdims=True)
                p = jnp.exp(s - m)
                denom = jnp.sum(p, axis=-1, keepdims=True) + jnp.exp(sink - m)
                o = _dot(p.astype(BF16), vband) / denom
                att = jnp.where(low_half, o[:WINDOW], o[WINDOW:])
                cat_scr[pl.ds(r0, WINDOW), slab * LANES:(slab + 1) * LANES] = att.astype(BF16)
        for j in range(SGU_GROUPS // 2):
            vp = sv_scr[pl.ds(r0, CHUNK), j * LANES:(j + 1) * LANES]
            blank = jnp.zeros_like(vp)
            rhs = jnp.concatenate([jnp.where(low_half, vp, blank),
                                   jnp.where(low_half, blank, vp)], axis=0)
            mixed = _dot(wcat_scr[j], rhs) + bs_ref[:, j * LANES:(j + 1) * LANES]
            out = u_scr[pl.ds(r0, CHUNK), j * LANES:(j + 1) * LANES] * mixed
            cat_scr[pl.ds(r0, CHUNK), Q_W + j * LANES:Q_W + (j + 1) * LANES] = out.astype(BF16)

    for bi in range(nblk):
        block(bi)

    return _dot(cat_scr[...], wout_v[...])


def _mixer_ab_call(x, pos, sinks, inv_lane, sgn_lane, sgu_g, sgu_b, w_s, bias_t, ln_g, ln_b,
                   alpha, li, ji, tm, seq, weights, casts):
    n, d = x.shape
    tiles = n // tm
    body = functools.partial(_mixer_ab_kernel, alpha, li, ji, tm, seq // tm, weights, casts)
    cur, prev = _staggered_maps(tiles)
    main_specs = [
        pl.BlockSpec(memory_space=pltpu.SMEM),
        pl.BlockSpec((tm, d), cur),
        pl.BlockSpec((tm, d), prev),
        pl.BlockSpec((tm, 1), cur),
        _resident((1, LANES)), _resident((1, LANES)),
        _resident(sgu_g.shape), _resident(sgu_b.shape),
        _resident(w_s.shape),
        _resident((CHUNK, SGU_W)),
        _resident(ln_g.shape), _resident(ln_b.shape),
    ]
    main_args = [sinks, x, x, pos, inv_lane, sgn_lane, sgu_g, sgu_b, w_s, bias_t, ln_g, ln_b]
    scratch = [
        pltpu.VMEM((2, tm, Q_W), BF16),
        pltpu.VMEM((2, tm + WINDOW, 4 * LANES), BF16),
        pltpu.VMEM((2, tm, SGU_W), F32),
        pltpu.VMEM((2, tm, SGU_W), BF16),
        pltpu.VMEM((SGU_GROUPS // 2, CHUNK, 2 * CHUNK), BF16),
        pltpu.VMEM((tm, Q_W + SGU_W), BF16),
    ]
    return _call(body, "mixer_attn_sgu", tiles, tm, d, main_specs, main_args, weights, casts,
                 scratch)


def _mixer_conv_kernel(alpha, li, ji, tm, tiles_per_seq, weights, casts, *refs):
    main, (win_v, wout_v), load, o_ref, scratch = _split_refs(refs, 4, weights, casts)
    x_ref, cw_ref, lng_ref, lnb_ref = main
    cz_scr, x_keep, mix_keep = scratch
    i = pl.program_id(0)
    d = x_ref.shape[1]
    pad = SUBLANES

    @pl.when((i % tiles_per_seq) == 0)
    def _reset_halo():
        cz_scr[0:pad, :] = jnp.zeros((pad, d), F32)

    def matmuls(zero):
        x = x_ref[...]
        x_keep[...] = x
        h = _dot(x.astype(BF16), win_v[...])
        cz = h[:, d:2 * d] * h[:, 2 * d:3 * d]
        cz_scr[pad:, :] = cz
        y = (cw_ref[ji, CONV_WIDTH - 1:CONV_WIDTH, :] + zero) * cz
        for t in range(CONV_WIDTH - 1):
            back = CONV_WIDTH - 1 - t
            y = y + cw_ref[ji, t:t + 1, :] * cz_scr[pad - back:pad - back + tm, :]
        cz_scr[0:pad, :] = cz_scr[tm:tm + pad, :]
        mix_keep[...] = _dot((h[:, 0:d] * y).astype(BF16), wout_v[...])

    def finish(chained):
        def store(rows, y):
            o_ref[rows, :] = y
        return _residual_ln(alpha, 1.0, x_keep, mix_keep,
                            lng_ref[li, 1:2, :], lnb_ref[li, 1:2, :], store, chained)

    _staggered(load, matmuls, finish)


def _mixer_conv_call(x, conv_w, ln_g, ln_b, alpha, li, ji, tm, seq, weights, casts):
    n, d = x.shape
    tiles = n // tm
    body = functools.partial(_mixer_conv_kernel, alpha, li, ji, tm, seq // tm, weights, casts)
    cur, _ = _staggered_maps(tiles)
    main_specs = [pl.BlockSpec((tm, d), cur), _resident(conv_w.shape),
                  _resident(ln_g.shape), _resident(ln_b.shape)]
    scratch = [pltpu.VMEM((tm + SUBLANES, d), F32),
               pltpu.VMEM((tm, d), F32), pltpu.VMEM((tm, d), F32)]
    return _call(body, "mixer_conv", tiles, tm, d, main_specs, [x, conv_w, ln_g, ln_b],
                 weights, casts, scratch)


def kernel(x, p, positions, ln_g, ln_b, ffn_w_gate, ffn_w_up, ffn_w_down, ab_w_in, ab_sinks,
           sgu_ln_g, sgu_ln_b, sgu_w_s, sgu_b_s, ab_w_out, sc_w_in, sc_conv_w, sc_w_out,
           ple_w_proj, ple_w_gate):
    bsz, seq, d = x.shape
    depth = p.shape[0]
    alpha = (2 * depth) ** 0.25
    tm = TM
    assert seq % tm == 0 and tm % WINDOW == 0
    assert sc_conv_w.shape[1] == CONV_WIDTH

    n = bsz * seq
    steps = n // tm + 1
    xs = x.reshape(n, d)
    pos = positions.reshape(n, 1)
    ps = p.reshape(depth, n, p.shape[-1])

    half = ROT_DIM // 2
    inv = jnp.power(ROPE_THETA, -jnp.arange(half, dtype=F32) * (2.0 / ROT_DIM))
    r = jnp.arange(LANES) % ROT_DIM
    inv_lane = inv[r % half].reshape(1, LANES)
    sgn_lane = jnp.where(r < half, -1.0, 1.0).reshape(1, LANES).astype(F32)

    calls = []
    for i in range(depth):
        j = i // 2
        ffn = lambda h, i=i: [(ffn_w_gate, (i, h)), (ffn_w_up, (i, h)), (ffn_w_down, (i, h))]
        calls.append(("ffn", i, 0, ffn(0)))
        if i % 2 == 0:
            calls.append(("attn", i, j, [(ab_w_in, (j,)), (ab_w_out, (j,))]))
        else:
            calls.append(("conv", i, j, [(sc_w_in, (j,)), (sc_w_out, (j,))]))
        calls.append(("ple", i, 1, ffn(1) + [(ple_w_proj, (i,)), (ple_w_gate, (i,))]))

    given = None
    for k, (kind, i, j, mats) in enumerate(calls):
        weights = _Weights(mats, given)
        casts = _Casts(calls[k + 1][3] if k + 1 < len(calls) else [], steps)
        if kind == "ffn":
            xs, given = _ffn_call(xs, ln_g, ln_b, alpha, i, j, tm, weights, casts)
        elif kind == "ple":
            xs, given = _ffn_call(xs, ln_g, ln_b, alpha, i, j, tm, weights, casts, p=ps)
        elif kind == "attn":
            bias_t = jnp.repeat(jnp.transpose(sgu_b_s[j]), SGU_GROUP_DIM, axis=1)
            xs, given = _mixer_ab_call(xs, pos, ab_sinks, inv_lane, sgn_lane, sgu_ln_g, sgu_ln_b,
                                       sgu_w_s, bias_t, ln_g, ln_b, alpha, i, j, tm, seq,
                                       weights, casts)
        else:
            xs, given = _mixer_conv_call(xs, sc_conv_w, ln_g, ln_b, alpha, i, j, tm, seq,
                                         weights, casts)
    return xs.reshape(bsz, seq, d)
```

```python
import functools
import math

import jax
import jax.numpy as jnp
from jax import lax
from jax.experimental import pallas as pl
from jax.experimental.pallas import tpu as pltpu

ATT_HEADS = 8
ATT_KV_HEADS = 2
HEAD_DIM = 64
WINDOW = 128
ROT_DIM = HEAD_DIM // 4
ROPE_THETA = 500000.0
SGU_GROUPS = 8
SGU_GROUP_DIM = 64
CHUNK = 128
CONV_WIDTH = 3
LN_EPS = 1e-5
NEG_INF = -1e30

Q_W = ATT_HEADS * HEAD_DIM
KV_W = ATT_KV_HEADS * HEAD_DIM
SGU_W = SGU_GROUPS * SGU_GROUP_DIM

LANES = 128
SUBLANES = 8
BF16_ROWS = 16
VMEM_LIMIT = 56 * 1024 * 1024
STAGE_BYTES = 3 * 512 * 1024
STAGE_SLOTS = 4
TM = 512
LN_ROWS = 8
LN_AHEAD = 6

BF16 = jnp.bfloat16
F32 = jnp.float32


def _dot(a, b):
    return jnp.dot(a, b, preferred_element_type=F32)


def _layer_norm(y, g, b):
    mu = jnp.mean(y, axis=-1, keepdims=True)
    d = y - mu
    var = jnp.mean(d * d, axis=-1, keepdims=True)
    return d * lax.rsqrt(var + LN_EPS) * g + b


def _gelu_tanh(x):
    c = math.sqrt(2.0 / math.pi)
    return x * (0.5 * (1.0 + jnp.tanh(c * (x + 0.044715 * (x * x * x)))))


def _resident(shape):
    nd = len(shape)
    return pl.BlockSpec(shape, lambda i: (0,) * nd, pipeline_mode=pl.Buffered(1))


_HBM = pl.BlockSpec(memory_space=pl.ANY)


def _params():
    return pltpu.CompilerParams(dimension_semantics=("arbitrary",),
                                vmem_limit_bytes=VMEM_LIMIT)


def _stage(cols):
    rows = 1 << ((STAGE_BYTES // (4 * cols)).bit_length() - 1)
    return [pltpu.VMEM((STAGE_SLOTS, rows, cols), F32), pltpu.SemaphoreType.DMA((STAGE_SLOTS,))]


def _fetch_cast(jobs):
    chunks = []
    used = {}
    for src, dst, stage, sem in jobs:
        total, cols = src.shape
        rows = min(stage.shape[1], total)
        assert total % rows == 0 and cols == stage.shape[2] and dst.shape == src.shape
        for c in range(total // rows):
            slot = used.get(id(stage), 0) % STAGE_SLOTS
            used[id(stage)] = used.get(id(stage), 0) + 1
            staged = stage.at[slot, pl.ds(0, rows), :]
            copy = pltpu.make_async_copy(src.at[pl.ds(c * rows, rows), :], staged, sem.at[slot])
            chunks.append((copy, staged, dst.at[pl.ds(c * rows, rows), :]))
    for copy, _, _ in chunks[:STAGE_SLOTS]:
        copy.start()
    for n, (copy, staged, out) in enumerate(chunks):
        copy.wait()
        out[...] = staged[...].astype(BF16)
        if n + STAGE_SLOTS < len(chunks):
            chunks[n + STAGE_SLOTS][0].start()


class _Weights:
    def __init__(self, mats, given):
        self.mats, self.given = mats, given
        self.shapes = [tuple(a.shape[-2:]) for a, _ in mats]
        self.widths = sorted({s[1] for s in self.shapes})

    @property
    def args(self):
        return list(self.given) if self.given else [a for a, _ in self.mats]

    @property
    def in_specs(self):
        return [_resident(s) for s in self.shapes] if self.given else [_HBM] * len(self.mats)

    @property
    def scratch(self):
        if self.given:
            return []
        out = [pltpu.VMEM(s, BF16) for s in self.shapes]
        for w in self.widths:
            out += _stage(w)
        return out

    def bind(self, in_refs, scratch_refs):
        if self.given:
            return list(in_refs), (lambda: None)
        n = len(self.mats)
        stages = {w: (scratch_refs[n + 2 * k], scratch_refs[n + 2 * k + 1])
                  for k, w in enumerate(self.widths)}
        jobs = [(src.at[lead], dst) + stages[shape[1]]
                for src, (_, lead), dst, shape in zip(in_refs, self.mats, scratch_refs, self.shapes)]
        return list(scratch_refs[:n]), (lambda: _fetch_cast(jobs))


class _Casts:
    def __init__(self, mats, steps):
        self.mats = mats
        self.blocks = []
        for a, _ in mats:
            rows, cols = a.shape[-2:]
            rb = next(r for r in range(BF16_ROWS, rows + 1, BF16_ROWS)
                      if rows % r == 0 and rows // r <= steps)
            self.blocks.append((rb, rows // rb, rows, cols))

    @property
    def args(self):
        return [a for a, _ in self.mats]

    @property
    def in_specs(self):
        return [pl.BlockSpec((None,) * len(lead) + (rb, cols),
                             lambda i, lead=lead, nb=nb: lead + (jnp.minimum(i, nb - 1), 0))
                for (_, lead), (rb, nb, _, cols) in zip(self.mats, self.blocks)]

    @property
    def out_specs(self):
        return [pl.BlockSpec((rb, cols), lambda i, nb=nb: (jnp.minimum(i, nb - 1), 0))
                for rb, nb, _, cols in self.blocks]

    @property
    def out_shapes(self):
        return [jax.ShapeDtypeStruct((rows, cols), BF16) for _, _, rows, cols in self.blocks]

    @staticmethod
    def run(in_refs, out_refs):
        for src, dst in zip(in_refs, out_refs):
            dst[...] = src[...].astype(BF16)


def _split_refs(refs, n_main, weights, casts):
    nw, nc = len(weights.mats), len(casts.mats)
    main, rest = refs[:n_main], refs[n_main:]
    w_in, c_in = rest[:nw], rest[nw:nw + nc]
    o_ref, c_out = rest[nw + nc], rest[nw + nc + 1:nw + 2 * nc + 1]
    scratch = rest[nw + 2 * nc + 1:]
    ns = len(weights.scratch)
    w, load = weights.bind(w_in, scratch[:ns])
    _Casts.run(c_in, c_out)
    return main, w, load, o_ref, scratch[ns:]


def _call(body, name, tiles, tm, d, main_specs, main_args, weights, casts, scratch):
    outs = pl.pallas_call(
        body,
        grid=(tiles + 1,),
        in_specs=main_specs + weights.in_specs + casts.in_specs,
        out_specs=[pl.BlockSpec((tm, d), _staggered_maps(tiles)[1])] + casts.out_specs,
        out_shape=[jax.ShapeDtypeStruct((tiles * tm, d), F32)] + casts.out_shapes,
        scratch_shapes=weights.scratch + scratch,
        compiler_params=_params(),
        name=name,
    )(*main_args, *weights.args, *casts.args)
    return outs[0], list(outs[1:])


def _fold_bits(y):
    bits = lax.bitcast_convert_type(y, jnp.int32)
    cols = bits[:, 0:LANES]
    for j in range(1, y.shape[1] // LANES):
        cols = cols | bits[:, j * LANES:(j + 1) * LANES]
    rows = cols[0:SUBLANES]
    for k in range(1, y.shape[0] // SUBLANES):
        rows = rows | cols[k * SUBLANES:(k + 1) * SUBLANES]
    return rows


def _zero_from(bits):
    top = jnp.max(bits, axis=(0, 1), keepdims=True)
    cleared = lax.shift_right_logical(lax.shift_right_logical(top, 16), 16)
    return lax.bitcast_convert_type(cleared, F32)


def _residual_ln(alpha, beta, x_keep, r_keep, g, b, store, chained):
    if not chained:
        beta_r = r_keep[...] if beta == 1.0 else beta * r_keep[...]
        store(slice(None), _layer_norm(alpha * x_keep[...] + beta_r, g, b))
        return None
    ngroups = x_keep.shape[0] // LN_ROWS
    width = x_keep.shape[1]
    folded = []
    for c in range(ngroups):
        rows = pl.ds(c * LN_ROWS, LN_ROWS)
        a = alpha
        if c >= LN_AHEAD:
            z = lax.shift_right_logical(lax.shift_right_logical(folded[c - LN_AHEAD], 16), 16)
            a = alpha + jnp.tile(lax.bitcast_convert_type(z, F32),
                                 (LN_ROWS // SUBLANES, width // LANES))
        r = r_keep[rows, :]
        y = _layer_norm(a * x_keep[rows, :] + (r if beta == 1.0 else beta * r), g, b)
        store(rows, y)
        folded.append(_fold_bits(y))
    bits = folded[0]
    for f in folded[1:]:
        bits = bits | f
    return bits


def _staggered_maps(tiles):
    return (lambda i: (jnp.minimum(i, tiles - 1), 0)), (lambda i: (jnp.maximum(i - 1, 0), 0))


def _staggered(load, matmuls, finish, after=lambda: None):
    i = pl.program_id(0)
    last = pl.num_programs(0) - 1

    @pl.when(i == 0)
    def _first():
        load()
        matmuls(0.0)

    @pl.when(jnp.logical_and(i > 0, i < last))
    def _steady():
        matmuls(_zero_from(finish(True)))
        after()

    @pl.when(i == last)
    def _last():
        finish(False)
        after()


def _ffn_matmuls(x_ref, wg_v, wu_v, wd_v, x_keep, ff_keep, zero):
    x = x_ref[...]
    xb = x.astype(BF16)
    gate = _dot(xb, wg_v[...])
    up = _dot(xb, wu_v[...])
    one = 1.0 + zero
    h = (gate * (one / (one + jnp.exp(-gate))) * up).astype(BF16)
    x_keep[...] = x
    ff_keep[...] = _dot(h, wd_v[...])


def _ffn_kernel(alpha, li, hi, ple, weights, casts, *refs):
    main, w, load, o_ref, (x_keep, ff_keep) = _split_refs(refs, 4 if ple else 3, weights, casts)
    x_ref, lng_ref, lnb_ref = main[:3]
    s = 2 * hi

    def store(rows, y):
        o_ref[rows, :] = y

    def finish(chained):
        return _residual_ln(alpha, 0.5, x_keep, ff_keep,
                            lng_ref[li, s:s + 1, :], lnb_ref[li, s:s + 1, :], store, chained)

    def embed():
        y = o_ref[...]
        emb = _dot(main[3][...].astype(BF16), w[3][...])
        gate = jax.nn.sigmoid(_dot(y.astype(BF16), w[4][...]))
        o_ref[...] = y + emb * gate

    matmuls = functools.partial(_ffn_matmuls, x_ref, w[0], w[1], w[2], x_keep, ff_keep)
    if ple:
        _staggered(load, matmuls, finish, embed)
    else:
        _staggered(load, matmuls, finish)


def _ffn_call(x, ln_g, ln_b, alpha, li, hi, tm, weights, casts, p=None):
    n, d = x.shape
    tiles = n // tm
    cur, prev = _staggered_maps(tiles)
    main_specs = [pl.BlockSpec((tm, d), cur), _resident(ln_g.shape), _resident(ln_b.shape)]
    main_args = [x, ln_g, ln_b]
    if p is not None:
        main_specs.append(pl.BlockSpec((None, tm, p.shape[-1]), lambda i: (li,) + prev(i)))
        main_args.append(p)
    body = functools.partial(_ffn_kernel, alpha, li, hi, p is not None, weights, casts)
    scratch = [pltpu.VMEM((tm, d), F32), pltpu.VMEM((tm, d), F32)]
    return _call(body, "ffn_ln" if p is None else "ffn_ln_ple", tiles, tm, d,
                 main_specs, main_args, weights, casts, scratch)


def _mixer_ab_kernel(alpha, li, ji, tm, tiles_per_seq, weights, casts, *refs):
    main, (win_v, wout_v), load_weights, o_ref, scratch = _split_refs(refs, 12, weights, casts)
    (sinks_ref, x_ref, xprev_ref, pos_ref, inv_ref, sgn_ref, sg_ref, sb_ref, ws_ref, bs_ref,
     lng_ref, lnb_ref) = main
    q_scr, kv_scr, u_scr, sv_scr, wcat_scr, cat_scr = scratch
    i = pl.program_id(0)
    last = pl.num_programs(0) - 1
    lane = lax.broadcasted_iota(jnp.int32, (1, LANES), 1)
    low_half = lane < HEAD_DIM
    slot = i % 2

    def load():
        load_weights()
        r = lax.broadcasted_iota(jnp.int32, (CHUNK, CHUNK), 0)
        c = lax.broadcasted_iota(jnp.int32, (CHUNK, CHUNK), 1)
        tril = r >= c
        for j in range(SGU_GROUPS // 2):
            a = jnp.where(tril, ws_ref[ji, 2 * j], 0.0)
            bb = jnp.where(tril, ws_ref[ji, 2 * j + 1], 0.0)
            wcat_scr[j] = jnp.concatenate([a, bb], axis=1).astype(BF16)

    def project():
        _mixer_ab_project(ji, tm, (i % tiles_per_seq) == 0, low_half, lane, x_ref, pos_ref,
                          win_v, inv_ref, sgn_ref, sg_ref, sb_ref,
                          q_scr.at[slot], kv_scr.at[slot], kv_scr.at[1 - slot],
                          u_scr.at[slot], sv_scr.at[slot])

    def attend():
        mix = _mixer_ab_attend(ji, tm, ((i - 1) % tiles_per_seq) == 0, low_half, sinks_ref,
                               bs_ref, wout_v, q_scr.at[1 - slot], kv_scr.at[1 - slot],
                               u_scr.at[1 - slot], sv_scr.at[1 - slot], wcat_scr, cat_scr)
        o_ref[...] = _layer_norm(alpha * xprev_ref[...] + mix,
                                 lng_ref[li, 1:2, :], lnb_ref[li, 1:2, :])

    @pl.when(i == 0)
    def _first():
        load()
        project()

    @pl.when(jnp.logical_and(i > 0, i < last))
    def _steady():
        attend()
        project()

    @pl.when(i == last)
    def _last():
        attend()


def _mixer_ab_project(ji, tm, first_tile, low_half, lane, x_ref, pos_ref, win_v,
                      inv_ref, sgn_ref, sg_ref, sb_ref, q_scr, kv_scr, kv_other, u_scr, sv_scr):
    x = x_ref[...]
    h = _dot(x.astype(BF16), win_v[...])

    nseg = LANES // ROT_DIM
    seg = tm // nseg
    pos = pos_ref[...].astype(F32)
    packed = jnp.zeros((seg, LANES), F32)
    for s in range(nseg):
        in_seg = (lane // ROT_DIM) == s
        packed = jnp.where(in_seg, pos[:, s:s + 1], packed)
    ang = packed * inv_ref[...]
    cos_p = jnp.cos(ang)
    sin_p = jnp.sin(ang) * sgn_ref[...]
    rot_lo = lane < ROT_DIM
    rot_hi = jnp.logical_and(lane >= HEAD_DIM, lane < HEAD_DIM + ROT_DIM)

    def unpack(t, fill):
        parts = []
        for s in range(nseg):
            lo = t if s == 0 else pltpu.roll(t, LANES - s * ROT_DIM, axis=1)
            shift = (HEAD_DIM - s * ROT_DIM) % LANES
            hi = t if shift == 0 else pltpu.roll(t, shift, axis=1)
            parts.append(jnp.where(rot_lo, lo, jnp.where(rot_hi, hi, fill)))
        return jnp.concatenate(parts, axis=0)

    cos_t = unpack(cos_p, 1.0)
    sin_t = unpack(sin_p, 0.0)
    take_up = (lane % HEAD_DIM) < (ROT_DIM // 2)

    def rotary(t):
        up = pltpu.roll(t, LANES - ROT_DIM // 2, axis=1)
        dn = pltpu.roll(t, ROT_DIM // 2, axis=1)
        return t * cos_t + jnp.where(take_up, up, dn) * sin_t

    scale = HEAD_DIM ** -0.5
    for j in range(Q_W // LANES):
        qj = rotary(h[:, j * LANES:(j + 1) * LANES]) * scale
        q_scr[:, j * LANES:(j + 1) * LANES] = qj.astype(BF16)

    def dup_heads(t):
        sw = pltpu.roll(t, HEAD_DIM, axis=1)
        return jnp.where(low_half, t, sw), jnp.where(low_half, sw, t)

    k0, k1 = dup_heads(rotary(h[:, Q_W:Q_W + KV_W]))
    v0, v1 = dup_heads(h[:, Q_W + KV_W:Q_W + 2 * KV_W])
    kv_scr[WINDOW:, 0 * LANES:1 * LANES] = k0.astype(BF16)
    kv_scr[WINDOW:, 1 * LANES:2 * LANES] = k1.astype(BF16)
    kv_scr[WINDOW:, 2 * LANES:3 * LANES] = v0.astype(BF16)
    kv_scr[WINDOW:, 3 * LANES:4 * LANES] = v1.astype(BF16)

    su0 = Q_W + 2 * KV_W
    u_scr[...] = _gelu_tanh(h[:, su0:su0 + SGU_W])
    sv = _layer_norm(_gelu_tanh(h[:, su0 + SGU_W:su0 + 2 * SGU_W]),
                     sg_ref[ji:ji + 1, :], sb_ref[ji:ji + 1, :])
    sv_scr[...] = sv.astype(BF16)

    halo = kv_other[tm:tm + WINDOW, :]
    kv_scr[0:WINDOW, :] = jnp.where(first_tile, jnp.zeros_like(halo), halo)


def _mixer_ab_attend(ji, tm, first_tile, low_half, sinks_ref, bs_ref, wout_v,
                     q_scr, kv_scr, u_scr, sv_scr, wcat_scr, cat_scr):
    nblk = tm // WINDOW
    qi = lax.broadcasted_iota(jnp.int32, (2 * WINDOW, 2 * WINDOW), 0) % WINDOW
    kj = lax.broadcasted_iota(jnp.int32, (2 * WINDOW, 2 * WINDOW), 1)
    upper_ok = kj <= qi + WINDOW
    row_first = lax.broadcasted_iota(jnp.int32, (2 * WINDOW, 1), 0) < WINDOW

    band_ok = jnp.logical_and(upper_ok, kj > qi)
    band_ok_first = jnp.logical_and(upper_ok, kj >= jnp.where(first_tile, WINDOW, qi + 1))

    def block(bi):
        r0 = bi * WINDOW
        valid = band_ok_first if bi == 0 else band_ok
        for kvh in range(ATT_KV_HEADS):
            kband = kv_scr[pl.ds(r0, 2 * WINDOW), kvh * LANES:(kvh + 1) * LANES]
            vband = kv_scr[pl.ds(r0, 2 * WINDOW), (2 + kvh) * LANES:(3 + kvh) * LANES]
            for pr in range(2):
                slab = kvh * 2 + pr
                qp = q_scr[pl.ds(r0, WINDOW), slab * LANES:(slab + 1) * LANES]
                blank = jnp.zeros_like(qp)
                qs = jnp.concatenate([jnp.where(low_half, qp, blank),
                                      jnp.where(low_half, blank, qp)], axis=0)
                s = lax.dot_general(qs, kband, (((1,), (1,)), ((), ())),
                                    preferred_element_type=F32)
                s = jnp.where(valid, s, NEG_INF)
                sink = jnp.where(row_first, sinks_ref[ji, 2 * slab], sinks_ref[ji, 2 * slab + 1])
                m = jnp.max(s, axis=-1, keepdims=True)
                p = jnp.exp(s - m)
                denom = jnp.sum(p, axis=-1, keepdims=True) + jnp.exp(sink - m)
                o = _dot(p.astype(BF16), vband) / denom
                att = jnp.where(low_half, o[:WINDOW], o[WINDOW:])
                cat_scr[pl.ds(r0, WINDOW), slab * LANES:(slab + 1) * LANES] = att.astype(BF16)
        for j in range(SGU_GROUPS // 2):
            vp = sv_scr[pl.ds(r0, CHUNK), j * LANES:(j + 1) * LANES]
            blank = jnp.zeros_like(vp)
            rhs = jnp.concatenate([jnp.where(low_half, vp, blank),
                                   jnp.where(low_half, blank, vp)], axis=0)
            mixed = _dot(wcat_scr[j], rhs) + bs_ref[:, j * LANES:(j + 1) * LANES]
            out = u_scr[pl.ds(r0, CHUNK), j * LANES:(j + 1) * LANES] * mixed
            cat_scr[pl.ds(r0, CHUNK), Q_W + j * LANES:Q_W + (j + 1) * LANES] = out.astype(BF16)

    for bi in range(nblk):
        block(bi)

    return _dot(cat_scr[...], wout_v[...])


def _mixer_ab_call(x, pos, sinks, inv_lane, sgn_lane, sgu_g, sgu_b, w_s, bias_t, ln_g, ln_b,
                   alpha, li, ji, tm, seq, weights, casts):
    n, d = x.shape
    tiles = n // tm
    body = functools.partial(_mixer_ab_kernel, alpha, li, ji, tm, seq // tm, weights, casts)
    cur, prev = _staggered_maps(tiles)
    main_specs = [
        pl.BlockSpec(memory_space=pltpu.SMEM),
        pl.BlockSpec((tm, d), cur),
        pl.BlockSpec((tm, d), prev),
        pl.BlockSpec((tm // pos.shape[1], pos.shape[1]), cur),
        _resident((1, LANES)), _resident((1, LANES)),
        _resident(sgu_g.shape), _resident(sgu_b.shape),
        _resident(w_s.shape),
        _resident((CHUNK, SGU_W)),
        _resident(ln_g.shape), _resident(ln_b.shape),
    ]
    main_args = [sinks, x, x, pos, inv_lane, sgn_lane, sgu_g, sgu_b, w_s, bias_t, ln_g, ln_b]
    scratch = [
        pltpu.VMEM((2, tm, Q_W), BF16),
        pltpu.VMEM((2, tm + WINDOW, 4 * LANES), BF16),
        pltpu.VMEM((2, tm, SGU_W), F32),
        pltpu.VMEM((2, tm, SGU_W), BF16),
        pltpu.VMEM((SGU_GROUPS // 2, CHUNK, 2 * CHUNK), BF16),
        pltpu.VMEM((tm, Q_W + SGU_W), BF16),
    ]
    return _call(body, "mixer_attn_sgu", tiles, tm, d, main_specs, main_args, weights, casts,
                 scratch)


def _mixer_conv_kernel(alpha, li, ji, tm, tiles_per_seq, weights, casts, *refs):
    main, (win_v, wout_v), load, o_ref, scratch = _split_refs(refs, 4, weights, casts)
    x_ref, cw_ref, lng_ref, lnb_ref = main
    cz_scr, x_keep, mix_keep = scratch
    i = pl.program_id(0)
    d = x_ref.shape[1]
    pad = SUBLANES

    @pl.when((i % tiles_per_seq) == 0)
    def _reset_halo():
        cz_scr[0:pad, :] = jnp.zeros((pad, d), F32)

    def matmuls(zero):
        x = x_ref[...]
        x_keep[...] = x
        h = _dot(x.astype(BF16), win_v[...])
        cz = h[:, d:2 * d] * h[:, 2 * d:3 * d]
        cz_scr[pad:, :] = cz
        y = (cw_ref[ji, CONV_WIDTH - 1:CONV_WIDTH, :] + zero) * cz
        for t in range(CONV_WIDTH - 1):
            back = CONV_WIDTH - 1 - t
            y = y + cw_ref[ji, t:t + 1, :] * cz_scr[pad - back:pad - back + tm, :]
        cz_scr[0:pad, :] = cz_scr[tm:tm + pad, :]
        mix_keep[...] = _dot((h[:, 0:d] * y).astype(BF16), wout_v[...])

    def finish(chained):
        def store(rows, y):
            o_ref[rows, :] = y
        return _residual_ln(alpha, 1.0, x_keep, mix_keep,
                            lng_ref[li, 1:2, :], lnb_ref[li, 1:2, :], store, chained)

    _staggered(load, matmuls, finish)


def _mixer_conv_call(x, conv_w, ln_g, ln_b, alpha, li, ji, tm, seq, weights, casts):
    n, d = x.shape
    tiles = n // tm
    body = functools.partial(_mixer_conv_kernel, alpha, li, ji, tm, seq // tm, weights, casts)
    cur, _ = _staggered_maps(tiles)
    main_specs = [pl.BlockSpec((tm, d), cur), _resident(conv_w.shape),
                  _resident(ln_g.shape), _resident(ln_b.shape)]
    scratch = [pltpu.VMEM((tm + SUBLANES, d), F32),
               pltpu.VMEM((tm, d), F32), pltpu.VMEM((tm, d), F32)]
    return _call(body, "mixer_conv", tiles, tm, d, main_specs, [x, conv_w, ln_g, ln_b],
                 weights, casts, scratch)


def kernel(x, p, positions, ln_g, ln_b, ffn_w_gate, ffn_w_up, ffn_w_down, ab_w_in, ab_sinks,
           sgu_ln_g, sgu_ln_b, sgu_w_s, sgu_b_s, ab_w_out, sc_w_in, sc_conv_w, sc_w_out,
           ple_w_proj, ple_w_gate):
    bsz, seq, d = x.shape
    depth = p.shape[0]
    alpha = (2 * depth) ** 0.25
    tm = TM
    assert seq % tm == 0 and tm % WINDOW == 0
    assert sc_conv_w.shape[1] == CONV_WIDTH

    n = bsz * seq
    steps = n // tm + 1
    xs = x.reshape(n, d)
    nseg = LANES // ROT_DIM
    pos = positions.reshape(n // tm, nseg, tm // nseg).transpose(0, 2, 1).reshape(n // nseg, nseg)
    ps = p.reshape(depth, n, p.shape[-1])

    half = ROT_DIM // 2
    r = jnp.arange(LANES) % ROT_DIM
    inv_lane = jnp.power(ROPE_THETA, -(r % half).astype(F32) * (2.0 / ROT_DIM)).reshape(1, LANES)
    sgn_lane = jnp.where(r < half, -1.0, 1.0).reshape(1, LANES).astype(F32)

    calls = []
    for i in range(depth):
        j = i // 2
        ffn = lambda h, i=i: [(ffn_w_gate, (i, h)), (ffn_w_up, (i, h)), (ffn_w_down, (i, h))]
        calls.append(("ffn", i, 0, ffn(0)))
        if i % 2 == 0:
            calls.append(("attn", i, j, [(ab_w_in, (j,)), (ab_w_out, (j,))]))
        else:
            calls.append(("conv", i, j, [(sc_w_in, (j,)), (sc_w_out, (j,))]))
        calls.append(("ple", i, 1, ffn(1) + [(ple_w_proj, (i,)), (ple_w_gate, (i,))]))

    given = None
    for k, (kind, i, j, mats) in enumerate(calls):
        weights = _Weights(mats, given)
        casts = _Casts(calls[k + 1][3] if k + 1 < len(calls) else [], steps)
        if kind == "ffn":
            xs, given = _ffn_call(xs, ln_g, ln_b, alpha, i, j, tm, weights, casts)
        elif kind == "ple":
            xs, given = _ffn_call(xs, ln_g, ln_b, alpha, i, j, tm, weights, casts, p=ps)
        elif kind == "attn":
            bias_t = jnp.repeat(jnp.transpose(sgu_b_s[j]), SGU_GROUP_DIM, axis=1)
            xs, given = _mixer_ab_call(xs, pos, ab_sinks, inv_lane, sgn_lane, sgu_ln_g, sgu_ln_b,
                                       sgu_w_s, bias_t, ln_g, ln_b, alpha, i, j, tm, seq,
                                       weights, casts)
        else:
            xs, given = _mixer_conv_call(xs, sc_conv_w, ln_g, ln_b, alpha, i, j, tm, seq,
                                         weights, casts)
    return xs.reshape(bsz, seq, d)
```

```python
import functools
import math

import jax
import jax.numpy as jnp
from jax import lax
from jax.experimental import pallas as pl
from jax.experimental.pallas import tpu as pltpu

ATT_HEADS = 8
ATT_KV_HEADS = 2
HEAD_DIM = 64
WINDOW = 128
ROT_DIM = HEAD_DIM // 4
ROPE_THETA = 500000.0
SGU_GROUPS = 8
SGU_GROUP_DIM = 64
CHUNK = 128
CONV_WIDTH = 3
LN_EPS = 1e-5
NEG_INF = -1e30

Q_W = ATT_HEADS * HEAD_DIM
KV_W = ATT_KV_HEADS * HEAD_DIM
SGU_W = SGU_GROUPS * SGU_GROUP_DIM

LANES = 128
SUBLANES = 8
BF16_ROWS = 16
VMEM_LIMIT = 56 * 1024 * 1024
STAGE_BYTES = 3 * 512 * 1024
STAGE_SLOTS = 6
TM = 512
LN_ROWS = 8
LN_AHEAD = 6

BF16 = jnp.bfloat16
F32 = jnp.float32


def _dot(a, b):
    return jnp.dot(a, b, preferred_element_type=F32)


def _layer_norm(y, g, b):
    mu = jnp.mean(y, axis=-1, keepdims=True)
    d = y - mu
    var = jnp.mean(d * d, axis=-1, keepdims=True)
    return d * lax.rsqrt(var + LN_EPS) * g + b


def _gelu_tanh(x):
    c = math.sqrt(2.0 / math.pi)
    half_x = 0.5 * x
    return half_x * jnp.tanh(x * ((c * 0.044715) * (x * x) + c)) + half_x


def _resident(shape):
    nd = len(shape)
    return pl.BlockSpec(shape, lambda i: (0,) * nd, pipeline_mode=pl.Buffered(1))


_HBM = pl.BlockSpec(memory_space=pl.ANY)


def _params():
    return pltpu.CompilerParams(dimension_semantics=("arbitrary",),
                                vmem_limit_bytes=VMEM_LIMIT)


def _stage(cols):
    rows = 1 << ((STAGE_BYTES // (4 * cols)).bit_length() - 1)
    return [pltpu.VMEM((STAGE_SLOTS, rows, cols), F32), pltpu.SemaphoreType.DMA((STAGE_SLOTS,))]


def _fetch_cast(jobs):
    chunks = []
    used = {}
    for src, dst, stage, sem in jobs:
        total, cols = src.shape
        rows = min(stage.shape[1], total)
        assert total % rows == 0 and cols == stage.shape[2] and dst.shape == src.shape
        for c in range(total // rows):
            slot = used.get(id(stage), 0) % STAGE_SLOTS
            used[id(stage)] = used.get(id(stage), 0) + 1
            staged = stage.at[slot, pl.ds(0, rows), :]
            copy = pltpu.make_async_copy(src.at[pl.ds(c * rows, rows), :], staged, sem.at[slot])
            chunks.append((copy, staged, dst.at[pl.ds(c * rows, rows), :]))
    for copy, _, _ in chunks[:STAGE_SLOTS]:
        copy.start()
    for n, (copy, staged, out) in enumerate(chunks):
        copy.wait()
        out[...] = staged[...].astype(BF16)
        if n + STAGE_SLOTS < len(chunks):
            chunks[n + STAGE_SLOTS][0].start()


class _Weights:
    def __init__(self, mats, given):
        self.mats, self.given = mats, given
        self.shapes = [tuple(a.shape[-2:]) for a, _ in mats]
        self.widths = sorted({s[1] for s in self.shapes})

    @property
    def args(self):
        return list(self.given) if self.given else [a for a, _ in self.mats]

    @property
    def in_specs(self):
        return [_resident(s) for s in self.shapes] if self.given else [_HBM] * len(self.mats)

    @property
    def scratch(self):
        if self.given:
            return []
        out = [pltpu.VMEM(s, BF16) for s in self.shapes]
        for w in self.widths:
            out += _stage(w)
        return out

    def bind(self, in_refs, scratch_refs):
        if self.given:
            return list(in_refs), (lambda: None)
        n = len(self.mats)
        stages = {w: (scratch_refs[n + 2 * k], scratch_refs[n + 2 * k + 1])
                  for k, w in enumerate(self.widths)}
        jobs = [(src.at[lead], dst) + stages[shape[1]]
                for src, (_, lead), dst, shape in zip(in_refs, self.mats, scratch_refs, self.shapes)]
        return list(scratch_refs[:n]), (lambda: _fetch_cast(jobs))


class _Casts:
    def __init__(self, mats, steps):
        self.mats = mats
        self.blocks = []
        for a, _ in mats:
            rows, cols = a.shape[-2:]
            rb = next(r for r in range(BF16_ROWS, rows + 1, BF16_ROWS)
                      if rows % r == 0 and rows // r <= steps)
            self.blocks.append((rb, rows // rb, rows, cols))

    @property
    def args(self):
        return [a for a, _ in self.mats]

    @property
    def in_specs(self):
        return [pl.BlockSpec((None,) * len(lead) + (rb, cols),
                             lambda i, lead=lead, nb=nb: lead + (jnp.minimum(i, nb - 1), 0))
                for (_, lead), (rb, nb, _, cols) in zip(self.mats, self.blocks)]

    @property
    def out_specs(self):
        return [pl.BlockSpec((rb, cols), lambda i, nb=nb: (jnp.minimum(i, nb - 1), 0))
                for rb, nb, _, cols in self.blocks]

    @property
    def out_shapes(self):
        return [jax.ShapeDtypeStruct((rows, cols), BF16) for _, _, rows, cols in self.blocks]

    @staticmethod
    def run(in_refs, out_refs):
        for src, dst in zip(in_refs, out_refs):
            dst[...] = src[...].astype(BF16)


def _split_refs(refs, n_main, weights, casts):
    nw, nc = len(weights.mats), len(casts.mats)
    main, rest = refs[:n_main], refs[n_main:]
    w_in, c_in = rest[:nw], rest[nw:nw + nc]
    o_ref, c_out = rest[nw + nc], rest[nw + nc + 1:nw + 2 * nc + 1]
    scratch = rest[nw + 2 * nc + 1:]
    ns = len(weights.scratch)
    w, load = weights.bind(w_in, scratch[:ns])
    _Casts.run(c_in, c_out)
    return main, w, load, o_ref, scratch[ns:]


def _call(body, name, tiles, tm, d, main_specs, main_args, weights, casts, scratch):
    outs = pl.pallas_call(
        body,
        grid=(tiles + 1,),
        in_specs=main_specs + weights.in_specs + casts.in_specs,
        out_specs=[pl.BlockSpec((tm, d), _staggered_maps(tiles)[1])] + casts.out_specs,
        out_shape=[jax.ShapeDtypeStruct((tiles * tm, d), F32)] + casts.out_shapes,
        scratch_shapes=weights.scratch + scratch,
        compiler_params=_params(),
        name=name,
    )(*main_args, *weights.args, *casts.args)
    return outs[0], list(outs[1:])


def _fold_bits(y):
    bits = lax.bitcast_convert_type(y, jnp.int32)
    cols = bits[:, 0:LANES]
    for j in range(1, y.shape[1] // LANES):
        cols = cols | bits[:, j * LANES:(j + 1) * LANES]
    rows = cols[0:SUBLANES]
    for k in range(1, y.shape[0] // SUBLANES):
        rows = rows | cols[k * SUBLANES:(k + 1) * SUBLANES]
    return rows


def _zero_from(bits):
    top = jnp.max(bits, axis=(0, 1), keepdims=True)
    cleared = lax.shift_right_logical(lax.shift_right_logical(top, 16), 16)
    return lax.bitcast_convert_type(cleared, F32)


def _residual_ln(alpha, beta, x_keep, r_keep, g, b, store, chained):
    if not chained:
        beta_r = r_keep[...] if beta == 1.0 else beta * r_keep[...]
        store(slice(None), _layer_norm(alpha * x_keep[...] + beta_r, g, b))
        return None
    ngroups = x_keep.shape[0] // LN_ROWS
    width = x_keep.shape[1]
    folded = []
    for c in range(ngroups):
        rows = pl.ds(c * LN_ROWS, LN_ROWS)
        a = alpha
        if c >= LN_AHEAD:
            z = lax.shift_right_logical(lax.shift_right_logical(folded[c - LN_AHEAD], 16), 16)
            a = alpha + jnp.tile(lax.bitcast_convert_type(z, F32),
                                 (LN_ROWS // SUBLANES, width // LANES))
        r = r_keep[rows, :]
        y = _layer_norm(a * x_keep[rows, :] + (r if beta == 1.0 else beta * r), g, b)
        store(rows, y)
        folded.append(_fold_bits(y))
    bits = folded[0]
    for f in folded[1:]:
        bits = bits | f
    return bits


def _staggered_maps(tiles):
    return (lambda i: (jnp.minimum(i, tiles - 1), 0)), (lambda i: (jnp.maximum(i - 1, 0), 0))


def _staggered(load, matmuls, finish, after=lambda: None):
    i = pl.program_id(0)
    last = pl.num_programs(0) - 1

    @pl.when(i == 0)
    def _first():
        load()
        matmuls(0.0)

    @pl.when(jnp.logical_and(i > 0, i < last))
    def _steady():
        matmuls(_zero_from(finish(True)))
        after()

    @pl.when(i == last)
    def _last():
        finish(False)
        after()


def _ffn_matmuls(x_ref, wg_v, wu_v, wd_v, ff_keep, zero):
    xb = x_ref[...].astype(BF16)
    gate = _dot(xb, wg_v[...])
    up = _dot(xb, wu_v[...])
    one = 1.0 + zero
    h = (gate * (one / (one + jnp.exp(-gate))) * up).astype(BF16)
    ff_keep[...] = _dot(h, wd_v[...])


def _ffn_kernel(alpha, li, hi, ple, weights, casts, *refs):
    main, w, load, o_ref, (ff_keep,) = _split_refs(refs, 5 if ple else 4, weights, casts)
    x_ref, xprev_ref, lng_ref, lnb_ref = main[:4]
    s = 2 * hi

    def store(rows, y):
        o_ref[rows, :] = y

    def finish(chained):
        return _residual_ln(alpha, 0.5, xprev_ref, ff_keep,
                            lng_ref[li, s:s + 1, :], lnb_ref[li, s:s + 1, :], store, chained)

    def embed():
        y = o_ref[...]
        emb = _dot(main[4][...].astype(BF16), w[3][...])
        gate = jax.nn.sigmoid(_dot(y.astype(BF16), w[4][...]))
        o_ref[...] = y + emb * gate

    matmuls = functools.partial(_ffn_matmuls, x_ref, w[0], w[1], w[2], ff_keep)
    if ple:
        _staggered(load, matmuls, finish, embed)
    else:
        _staggered(load, matmuls, finish)


def _ffn_call(x, ln_g, ln_b, alpha, li, hi, tm, weights, casts, p=None):
    n, d = x.shape
    tiles = n // tm
    cur, prev = _staggered_maps(tiles)
    main_specs = [pl.BlockSpec((tm, d), cur), pl.BlockSpec((tm, d), prev),
                  _resident(ln_g.shape), _resident(ln_b.shape)]
    main_args = [x, x, ln_g, ln_b]
    if p is not None:
        main_specs.append(pl.BlockSpec((None, tm, p.shape[-1]), lambda i: (li,) + prev(i)))
        main_args.append(p)
    body = functools.partial(_ffn_kernel, alpha, li, hi, p is not None, weights, casts)
    scratch = [pltpu.VMEM((tm, d), F32)]
    return _call(body, "ffn_ln" if p is None else "ffn_ln_ple", tiles, tm, d,
                 main_specs, main_args, weights, casts, scratch)


def _mixer_ab_kernel(alpha, li, ji, tm, tiles_per_seq, weights, casts, *refs):
    main, (win_v, wout_v), load_weights, o_ref, scratch = _split_refs(refs, 12, weights, casts)
    (sinks_ref, x_ref, xprev_ref, pos_ref, inv_ref, sgn_ref, sg_ref, sb_ref, ws_ref, bs_ref,
     lng_ref, lnb_ref) = main
    q_scr, kv_scr, u_scr, sv_scr, wcat_scr, cat_scr = scratch
    i = pl.program_id(0)
    last = pl.num_programs(0) - 1
    lane = lax.broadcasted_iota(jnp.int32, (1, LANES), 1)
    low_half = lane < HEAD_DIM
    slot = i % 2

    def load():
        load_weights()
        r = lax.broadcasted_iota(jnp.int32, (CHUNK, CHUNK), 0)
        c = lax.broadcasted_iota(jnp.int32, (CHUNK, CHUNK), 1)
        tril = r >= c
        for j in range(SGU_GROUPS // 2):
            a = jnp.where(tril, ws_ref[ji, 2 * j], 0.0)
            bb = jnp.where(tril, ws_ref[ji, 2 * j + 1], 0.0)
            wcat_scr[j] = jnp.concatenate([a, bb], axis=1).astype(BF16)

    def project():
        _mixer_ab_project(ji, tm, (i % tiles_per_seq) == 0, low_half, lane, x_ref, pos_ref,
                          win_v, inv_ref, sgn_ref, sg_ref, sb_ref,
                          q_scr.at[slot], kv_scr.at[slot], kv_scr.at[1 - slot],
                          u_scr.at[slot], sv_scr.at[slot])

    def attend():
        mix = _mixer_ab_attend(ji, tm, ((i - 1) % tiles_per_seq) == 0, low_half, sinks_ref,
                               bs_ref, wout_v, q_scr.at[1 - slot], kv_scr.at[1 - slot],
                               u_scr.at[1 - slot], sv_scr.at[1 - slot], wcat_scr, cat_scr)
        o_ref[...] = _layer_norm(alpha * xprev_ref[...] + mix,
                                 lng_ref[li, 1:2, :], lnb_ref[li, 1:2, :])

    @pl.when(i == 0)
    def _first():
        load()
        project()

    @pl.when(jnp.logical_and(i > 0, i < last))
    def _steady():
        attend()
        project()

    @pl.when(i == last)
    def _last():
        attend()


def _mixer_ab_project(ji, tm, first_tile, low_half, lane, x_ref, pos_ref, win_v,
                      inv_ref, sgn_ref, sg_ref, sb_ref, q_scr, kv_scr, kv_other, u_scr, sv_scr):
    x = x_ref[...]
    h = _dot(x.astype(BF16), win_v[...])

    nseg = LANES // ROT_DIM
    seg = tm // nseg
    pos = pos_ref[...].astype(F32)
    packed = jnp.zeros((seg, LANES), F32)
    for s in range(nseg):
        in_seg = (lane // ROT_DIM) == s
        packed = jnp.where(in_seg, pos[:, s:s + 1], packed)
    ang = packed * inv_ref[...]
    cos_p = jnp.cos(ang)
    sin_p = jnp.sin(ang) * sgn_ref[...]
    rot_lo = lane < ROT_DIM
    rot_hi = jnp.logical_and(lane >= HEAD_DIM, lane < HEAD_DIM + ROT_DIM)

    def unpack(t, fill):
        parts = []
        for s in range(nseg):
            lo = t if s == 0 else pltpu.roll(t, LANES - s * ROT_DIM, axis=1)
            shift = (HEAD_DIM - s * ROT_DIM) % LANES
            hi = t if shift == 0 else pltpu.roll(t, shift, axis=1)
            parts.append(jnp.where(rot_lo, lo, jnp.where(rot_hi, hi, fill)))
        return jnp.concatenate(parts, axis=0)

    cos_t = unpack(cos_p, 1.0)
    sin_t = unpack(sin_p, 0.0)
    take_up = (lane % HEAD_DIM) < (ROT_DIM // 2)

    def rotary(t):
        up = pltpu.roll(t, LANES - ROT_DIM // 2, axis=1)
        dn = pltpu.roll(t, ROT_DIM // 2, axis=1)
        return t * cos_t + jnp.where(take_up, up, dn) * sin_t

    scale = HEAD_DIM ** -0.5
    for j in range(Q_W // LANES):
        qj = rotary(h[:, j * LANES:(j + 1) * LANES]) * scale
        q_scr[:, j * LANES:(j + 1) * LANES] = qj.astype(BF16)

    def dup_heads(t):
        sw = pltpu.roll(t, HEAD_DIM, axis=1)
        return jnp.where(low_half, t, sw), jnp.where(low_half, sw, t)

    k0, k1 = dup_heads(rotary(h[:, Q_W:Q_W + KV_W]))
    v0, v1 = dup_heads(h[:, Q_W + KV_W:Q_W + 2 * KV_W])
    kv_scr[WINDOW:, 0 * LANES:1 * LANES] = k0.astype(BF16)
    kv_scr[WINDOW:, 1 * LANES:2 * LANES] = k1.astype(BF16)
    kv_scr[WINDOW:, 2 * LANES:3 * LANES] = v0.astype(BF16)
    kv_scr[WINDOW:, 3 * LANES:4 * LANES] = v1.astype(BF16)

    su0 = Q_W + 2 * KV_W
    u_scr[...] = _gelu_tanh(h[:, su0:su0 + SGU_W])
    sv = _layer_norm(_gelu_tanh(h[:, su0 + SGU_W:su0 + 2 * SGU_W]),
                     sg_ref[ji:ji + 1, :], sb_ref[ji:ji + 1, :])
    sv_scr[...] = sv.astype(BF16)

    halo = kv_other[tm:tm + WINDOW, :]
    kv_scr[0:WINDOW, :] = jnp.where(first_tile, jnp.zeros_like(halo), halo)


def _mixer_ab_attend(ji, tm, first_tile, low_half, sinks_ref, bs_ref, wout_v,
                     q_scr, kv_scr, u_scr, sv_scr, wcat_scr, cat_scr):
    nblk = tm // WINDOW
    qi = lax.broadcasted_iota(jnp.int32, (2 * WINDOW, 2 * WINDOW), 0) % WINDOW
    kj = lax.broadcasted_iota(jnp.int32, (2 * WINDOW, 2 * WINDOW), 1)
    upper_ok = kj <= qi + WINDOW
    row_first = lax.broadcasted_iota(jnp.int32, (2 * WINDOW, 1), 0) < WINDOW

    band_ok = jnp.logical_and(upper_ok, kj > qi)
    band_ok_first = jnp.logical_and(upper_ok, kj >= jnp.where(first_tile, WINDOW, qi + 1))

    def block(bi):
        r0 = bi * WINDOW
        valid = band_ok_first if bi == 0 else band_ok
        for kvh in range(ATT_KV_HEADS):
            kband = kv_scr[pl.ds(r0, 2 * WINDOW), kvh * LANES:(kvh + 1) * LANES]
            vband = kv_scr[pl.ds(r0, 2 * WINDOW), (2 + kvh) * LANES:(3 + kvh) * LANES]
            for pr in range(2):
                slab = kvh * 2 + pr
                qp = q_scr[pl.ds(r0, WINDOW), slab * LANES:(slab + 1) * LANES]
                blank = jnp.zeros_like(qp)
                qs = jnp.concatenate([jnp.where(low_half, qp, blank),
                                      jnp.where(low_half, blank, qp)], axis=0)
                s = lax.dot_general(qs, kband, (((1,), (1,)), ((), ())),
                                    preferred_element_type=F32)
                s = jnp.where(valid, s, NEG_INF)
                sink = jnp.where(row_first, sinks_ref[ji, 2 * slab], sinks_ref[ji, 2 * slab + 1])
                m = jnp.max(s, axis=-1, keepdims=True)
                p = jnp.exp(s - m)
                denom = jnp.sum(p, axis=-1, keepdims=True) + jnp.exp(sink - m)
                o = _dot(p.astype(BF16), vband) / denom
                att = jnp.where(low_half, o[:WINDOW], o[WINDOW:])
                cat_scr[pl.ds(r0, WINDOW), slab * LANES:(slab + 1) * LANES] = att.astype(BF16)
        for j in range(SGU_GROUPS // 2):
            vp = sv_scr[pl.ds(r0, CHUNK), j * LANES:(j + 1) * LANES]
            blank = jnp.zeros_like(vp)
            rhs = jnp.concatenate([jnp.where(low_half, vp, blank),
                                   jnp.where(low_half, blank, vp)], axis=0)
            mixed = _dot(wcat_scr[j], rhs) + bs_ref[:, j * LANES:(j + 1) * LANES]
            out = u_scr[pl.ds(r0, CHUNK), j * LANES:(j + 1) * LANES] * mixed
            cat_scr[pl.ds(r0, CHUNK), Q_W + j * LANES:Q_W + (j + 1) * LANES] = out.astype(BF16)

    for bi in range(nblk):
        block(bi)

    return _dot(cat_scr[...], wout_v[...])


def _mixer_ab_call(x, pos, sinks, inv_lane, sgn_lane, sgu_g, sgu_b, w_s, bias_t, ln_g, ln_b,
                   alpha, li, ji, tm, seq, weights, casts):
    n, d = x.shape
    tiles = n // tm
    body = functools.partial(_mixer_ab_kernel, alpha, li, ji, tm, seq // tm, weights, casts)
    cur, prev = _staggered_maps(tiles)
    main_specs = [
        pl.BlockSpec(memory_space=pltpu.SMEM),
        pl.BlockSpec((tm, d), cur),
        pl.BlockSpec((tm, d), prev),
        pl.BlockSpec((tm // pos.shape[1], pos.shape[1]), cur),
        _resident((1, LANES)), _resident((1, LANES)),
        _resident(sgu_g.shape), _resident(sgu_b.shape),
        _resident(w_s.shape),
        _resident((CHUNK, SGU_W)),
        _resident(ln_g.shape), _resident(ln_b.shape),
    ]
    main_args = [sinks, x, x, pos, inv_lane, sgn_lane, sgu_g, sgu_b, w_s, bias_t, ln_g, ln_b]
    scratch = [
        pltpu.VMEM((2, tm, Q_W), BF16),
        pltpu.VMEM((2, tm + WINDOW, 4 * LANES), BF16),
        pltpu.VMEM((2, tm, SGU_W), F32),
        pltpu.VMEM((2, tm, SGU_W), BF16),
        pltpu.VMEM((SGU_GROUPS // 2, CHUNK, 2 * CHUNK), BF16),
        pltpu.VMEM((tm, Q_W + SGU_W), BF16),
    ]
    return _call(body, "mixer_attn_sgu", tiles, tm, d, main_specs, main_args, weights, casts,
                 scratch)


def _mixer_conv_kernel(alpha, li, ji, tm, tiles_per_seq, weights, casts, *refs):
    main, (win_v, wout_v), load, o_ref, scratch = _split_refs(refs, 5, weights, casts)
    x_ref, xprev_ref, cw_ref, lng_ref, lnb_ref = main
    cz_scr, mix_keep = scratch
    i = pl.program_id(0)
    d = x_ref.shape[1]
    pad = SUBLANES

    @pl.when((i % tiles_per_seq) == 0)
    def _reset_halo():
        cz_scr[0:pad, :] = jnp.zeros((pad, d), F32)

    def matmuls(zero):
        h = _dot(x_ref[...].astype(BF16), win_v[...])
        cz = h[:, d:2 * d] * h[:, 2 * d:3 * d]
        cz_scr[pad:, :] = cz
        y = (cw_ref[ji, CONV_WIDTH - 1:CONV_WIDTH, :] + zero) * cz
        for t in range(CONV_WIDTH - 1):
            back = CONV_WIDTH - 1 - t
            y = y + cw_ref[ji, t:t + 1, :] * cz_scr[pad - back:pad - back + tm, :]
        cz_scr[0:pad, :] = cz_scr[tm:tm + pad, :]
        mix_keep[...] = _dot((h[:, 0:d] * y).astype(BF16), wout_v[...])

    def finish(chained):
        def store(rows, y):
            o_ref[rows, :] = y
        return _residual_ln(alpha, 1.0, xprev_ref, mix_keep,
                            lng_ref[li, 1:2, :], lnb_ref[li, 1:2, :], store, chained)

    _staggered(load, matmuls, finish)


def _mixer_conv_call(x, conv_w, ln_g, ln_b, alpha, li, ji, tm, seq, weights, casts):
    n, d = x.shape
    tiles = n // tm
    body = functools.partial(_mixer_conv_kernel, alpha, li, ji, tm, seq // tm, weights, casts)
    cur, prev = _staggered_maps(tiles)
    main_specs = [pl.BlockSpec((tm, d), cur), pl.BlockSpec((tm, d), prev),
                  _resident(conv_w.shape), _resident(ln_g.shape), _resident(ln_b.shape)]
    scratch = [pltpu.VMEM((tm + SUBLANES, d), F32),
               pltpu.VMEM((tm, d), F32)]
    return _call(body, "mixer_conv", tiles, tm, d, main_specs, [x, x, conv_w, ln_g, ln_b],
                 weights, casts, scratch)


def kernel(x, p, positions, ln_g, ln_b, ffn_w_gate, ffn_w_up, ffn_w_down, ab_w_in, ab_sinks,
           sgu_ln_g, sgu_ln_b, sgu_w_s, sgu_b_s, ab_w_out, sc_w_in, sc_conv_w, sc_w_out,
           ple_w_proj, ple_w_gate):
    bsz, seq, d = x.shape
    depth = p.shape[0]
    alpha = (2 * depth) ** 0.25
    tm = TM
    assert seq % tm == 0 and tm % WINDOW == 0
    assert sc_conv_w.shape[1] == CONV_WIDTH

    n = bsz * seq
    steps = n // tm + 1
    xs = x.reshape(n, d)
    nseg = LANES // ROT_DIM
    pos = positions.reshape(n // tm, nseg, tm // nseg).transpose(0, 2, 1).reshape(n // nseg, nseg)
    ps = p.reshape(depth, n, p.shape[-1])

    half = ROT_DIM // 2
    r = jnp.arange(LANES) % ROT_DIM
    inv_lane = jnp.power(ROPE_THETA, -(r % half).astype(F32) * (2.0 / ROT_DIM)).reshape(1, LANES)
    sgn_lane = jnp.where(r < half, -1.0, 1.0).reshape(1, LANES).astype(F32)

    calls = []
    for i in range(depth):
        j = i // 2
        ffn = lambda h, i=i: [(ffn_w_gate, (i, h)), (ffn_w_up, (i, h)), (ffn_w_down, (i, h))]
        calls.append(("ffn", i, 0, ffn(0)))
        if i % 2 == 0:
            calls.append(("attn", i, j, [(ab_w_in, (j,)), (ab_w_out, (j,))]))
        else:
            calls.append(("conv", i, j, [(sc_w_in, (j,)), (sc_w_out, (j,))]))
        calls.append(("ple", i, 1, ffn(1) + [(ple_w_proj, (i,)), (ple_w_gate, (i,))]))

    given = None
    for k, (kind, i, j, mats) in enumerate(calls):
        weights = _Weights(mats, given)
        casts = _Casts(calls[k + 1][3] if k + 1 < len(calls) else [], steps)
        if kind == "ffn":
            xs, given = _ffn_call(xs, ln_g, ln_b, alpha, i, j, tm, weights, casts)
        elif kind == "ple":
            xs, given = _ffn_call(xs, ln_g, ln_b, alpha, i, j, tm, weights, casts, p=ps)
        elif kind == "attn":
            bias_t = jnp.repeat(jnp.transpose(sgu_b_s[j]), SGU_GROUP_DIM, axis=1)
            xs, given = _mixer_ab_call(xs, pos, ab_sinks, inv_lane, sgn_lane, sgu_ln_g, sgu_ln_b,
                                       sgu_w_s, bias_t, ln_g, ln_b, alpha, i, j, tm, seq,
                                       weights, casts)
        else:
            xs, given = _mixer_conv_call(xs, sc_conv_w, ln_g, ln_b, alpha, i, j, tm, seq,
                                         weights, casts)
    return xs.reshape(bsz, seq, d)
```

```python
import functools
import math

import jax
import jax.numpy as jnp
from jax import lax
from jax.experimental import pallas as pl
from jax.experimental.pallas import tpu as pltpu

ATT_HEADS = 8
ATT_KV_HEADS = 2
HEAD_DIM = 64
WINDOW = 128
ROT_DIM = HEAD_DIM // 4
ROPE_THETA = 500000.0
SGU_GROUPS = 8
SGU_GROUP_DIM = 64
CHUNK = 128
CONV_WIDTH = 3
LN_EPS = 1e-5
NEG_INF = -1e30

Q_W = ATT_HEADS * HEAD_DIM
KV_W = ATT_KV_HEADS * HEAD_DIM
SGU_W = SGU_GROUPS * SGU_GROUP_DIM

LANES = 128
SUBLANES = 8
BF16_ROWS = 16
VMEM_LIMIT = 56 * 1024 * 1024
STAGE_BYTES = 3 * 512 * 1024
STAGE_SLOTS = 6
TM = 512
EMBED_PARTS = 2
OUT_PARTS = 2
LN_ROWS = 8
LN_AHEAD = 6

BF16 = jnp.bfloat16
F32 = jnp.float32


def _dot(a, b):
    return jnp.dot(a, b, preferred_element_type=F32)


def _layer_norm(y, g, b):
    mu = jnp.mean(y, axis=-1, keepdims=True)
    d = y - mu
    var = jnp.mean(d * d, axis=-1, keepdims=True)
    return d * lax.rsqrt(var + LN_EPS) * g + b


def _gelu_tanh(x):
    c = math.sqrt(2.0 / math.pi)
    half_x = 0.5 * x
    return half_x * jnp.tanh(x * ((c * 0.044715) * (x * x) + c)) + half_x


def _resident(shape):
    nd = len(shape)
    return pl.BlockSpec(shape, lambda i: (0,) * nd, pipeline_mode=pl.Buffered(1))


_HBM = pl.BlockSpec(memory_space=pl.ANY)


def _params():
    return pltpu.CompilerParams(dimension_semantics=("arbitrary",),
                                vmem_limit_bytes=VMEM_LIMIT)


def _stage(cols):
    rows = 1 << ((STAGE_BYTES // (4 * cols)).bit_length() - 1)
    return [pltpu.VMEM((STAGE_SLOTS, rows, cols), F32), pltpu.SemaphoreType.DMA((STAGE_SLOTS,))]


def _fetch_cast(jobs):
    chunks = []
    used = {}
    for src, dst, stage, sem in jobs:
        total, cols = src.shape
        rows = min(stage.shape[1], total)
        assert total % rows == 0 and cols == stage.shape[2] and dst.shape == src.shape
        for c in range(total // rows):
            slot = used.get(id(stage), 0) % STAGE_SLOTS
            used[id(stage)] = used.get(id(stage), 0) + 1
            staged = stage.at[slot, pl.ds(0, rows), :]
            copy = pltpu.make_async_copy(src.at[pl.ds(c * rows, rows), :], staged, sem.at[slot])
            chunks.append((copy, staged, dst.at[pl.ds(c * rows, rows), :]))
    for copy, _, _ in chunks[:STAGE_SLOTS]:
        copy.start()
    for n, (copy, staged, out) in enumerate(chunks):
        copy.wait()
        out[...] = staged[...].astype(BF16)
        if n + STAGE_SLOTS < len(chunks):
            chunks[n + STAGE_SLOTS][0].start()


class _Weights:
    def __init__(self, mats, given):
        self.mats, self.given = mats, given
        self.shapes = [tuple(a.shape[-2:]) for a, _ in mats]
        self.widths = sorted({s[1] for s in self.shapes})

    @property
    def args(self):
        return list(self.given) if self.given else [a for a, _ in self.mats]

    @property
    def in_specs(self):
        return [_resident(s) for s in self.shapes] if self.given else [_HBM] * len(self.mats)

    @property
    def scratch(self):
        if self.given:
            return []
        out = [pltpu.VMEM(s, BF16) for s in self.shapes]
        for w in self.widths:
            out += _stage(w)
        return out

    def bind(self, in_refs, scratch_refs):
        if self.given:
            return list(in_refs), (lambda: None)
        n = len(self.mats)
        stages = {w: (scratch_refs[n + 2 * k], scratch_refs[n + 2 * k + 1])
                  for k, w in enumerate(self.widths)}
        jobs = [(src.at[lead], dst) + stages[shape[1]]
                for src, (_, lead), dst, shape in zip(in_refs, self.mats, scratch_refs, self.shapes)]
        return list(scratch_refs[:n]), (lambda: _fetch_cast(jobs))


class _Casts:
    def __init__(self, mats, steps):
        self.mats = mats
        self.blocks = []
        for a, _ in mats:
            rows, cols = a.shape[-2:]
            rb = next(r for r in range(BF16_ROWS, rows + 1, BF16_ROWS)
                      if rows % r == 0 and rows // r <= steps)
            self.blocks.append((rb, rows // rb, rows, cols))

    @property
    def args(self):
        return [a for a, _ in self.mats]

    @property
    def in_specs(self):
        return [pl.BlockSpec((None,) * len(lead) + (rb, cols),
                             lambda i, lead=lead, nb=nb: lead + (jnp.minimum(i, nb - 1), 0))
                for (_, lead), (rb, nb, _, cols) in zip(self.mats, self.blocks)]

    @property
    def out_specs(self):
        return [pl.BlockSpec((rb, cols), lambda i, nb=nb: (jnp.minimum(i, nb - 1), 0))
                for rb, nb, _, cols in self.blocks]

    @property
    def out_shapes(self):
        return [jax.ShapeDtypeStruct((rows, cols), BF16) for _, _, rows, cols in self.blocks]

    @staticmethod
    def run(in_refs, out_refs):
        for src, dst in zip(in_refs, out_refs):
            dst[...] = src[...].astype(BF16)


def _split_refs(refs, n_main, weights, casts):
    nw, nc = len(weights.mats), len(casts.mats)
    main, rest = refs[:n_main], refs[n_main:]
    w_in, c_in = rest[:nw], rest[nw:nw + nc]
    o_ref, c_out = rest[nw + nc], rest[nw + nc + 1:nw + 2 * nc + 1]
    scratch = rest[nw + 2 * nc + 1:]
    ns = len(weights.scratch)
    w, load = weights.bind(w_in, scratch[:ns])
    _Casts.run(c_in, c_out)
    return main, w, load, o_ref, scratch[ns:]


def _call(body, name, tiles, tm, d, main_specs, main_args, weights, casts, scratch):
    outs = pl.pallas_call(
        body,
        grid=(tiles + 1,),
        in_specs=main_specs + weights.in_specs + casts.in_specs,
        out_specs=[pl.BlockSpec((tm, d), _staggered_maps(tiles)[1])] + casts.out_specs,
        out_shape=[jax.ShapeDtypeStruct((tiles * tm, d), F32)] + casts.out_shapes,
        scratch_shapes=weights.scratch + scratch,
        compiler_params=_params(),
        name=name,
    )(*main_args, *weights.args, *casts.args)
    return outs[0], list(outs[1:])


def _fold_bits(y):
    bits = lax.bitcast_convert_type(y, jnp.int32)
    cols = bits[:, 0:LANES]
    for j in range(1, y.shape[1] // LANES):
        cols = cols | bits[:, j * LANES:(j + 1) * LANES]
    rows = cols[0:SUBLANES]
    for k in range(1, y.shape[0] // SUBLANES):
        rows = rows | cols[k * SUBLANES:(k + 1) * SUBLANES]
    return rows


def _zero_from(bits):
    top = jnp.max(bits, axis=(0, 1), keepdims=True)
    cleared = lax.shift_right_logical(lax.shift_right_logical(top, 16), 16)
    return lax.bitcast_convert_type(cleared, F32)


def _residual_ln(alpha, beta, x_keep, r_keep, g, b, store, chained):
    if not chained:
        beta_r = r_keep[...] if beta == 1.0 else beta * r_keep[...]
        store(slice(None), _layer_norm(alpha * x_keep[...] + beta_r, g, b))
        return None
    ngroups = x_keep.shape[0] // LN_ROWS
    width = x_keep.shape[1]
    folded = []
    for c in range(ngroups):
        rows = pl.ds(c * LN_ROWS, LN_ROWS)
        a = alpha
        if c >= LN_AHEAD:
            z = lax.shift_right_logical(lax.shift_right_logical(folded[c - LN_AHEAD], 16), 16)
            a = alpha + jnp.tile(lax.bitcast_convert_type(z, F32),
                                 (LN_ROWS // SUBLANES, width // LANES))
        r = r_keep[rows, :]
        y = _layer_norm(a * x_keep[rows, :] + (r if beta == 1.0 else beta * r), g, b)
        store(rows, y)
        folded.append(_fold_bits(y))
    bits = folded[0]
    for f in folded[1:]:
        bits = bits | f
    return bits


def _staggered_maps(tiles):
    return (lambda i: (jnp.minimum(i, tiles - 1), 0)), (lambda i: (jnp.maximum(i - 1, 0), 0))


def _staggered(load, matmuls, finish, after=lambda: None):
    i = pl.program_id(0)
    last = pl.num_programs(0) - 1

    @pl.when(i == 0)
    def _first():
        load()
        matmuls(0.0)

    @pl.when(jnp.logical_and(i > 0, i < last))
    def _steady():
        matmuls(_zero_from(finish(True)))
        after()

    @pl.when(i == last)
    def _last():
        finish(False)
        after()


def _ffn_matmuls(x_ref, wg_v, wu_v, wd_v, ff_keep, zero):
    xb = x_ref[...].astype(BF16)
    gate = _dot(xb, wg_v[...])
    up = _dot(xb, wu_v[...])
    one = 1.0 + zero
    h = (gate * (one / (one + jnp.exp(-gate))) * up).astype(BF16)
    ff_keep[...] = _dot(h, wd_v[...])


def _ffn_kernel(alpha, li, hi, ple, weights, casts, *refs):
    main, w, load, o_ref, (ff_keep,) = _split_refs(refs, 5 if ple else 4, weights, casts)
    x_ref, xprev_ref, lng_ref, lnb_ref = main[:4]
    s = 2 * hi

    def store(rows, y):
        o_ref[rows, :] = y

    def finish(chained):
        return _residual_ln(alpha, 0.5, xprev_ref, ff_keep,
                            lng_ref[li, s:s + 1, :], lnb_ref[li, s:s + 1, :], store, chained)

    def embed():
        part = o_ref.shape[0] // EMBED_PARTS
        for k in range(EMBED_PARTS):
            rows = pl.ds(k * part, part)
            y = o_ref[rows, :]
            emb = _dot(main[4][rows, :].astype(BF16), w[3][...])
            gate = jax.nn.sigmoid(_dot(y.astype(BF16), w[4][...]))
            o_ref[rows, :] = y + emb * gate

    matmuls = functools.partial(_ffn_matmuls, x_ref, w[0], w[1], w[2], ff_keep)
    if ple:
        _staggered(load, matmuls, finish, embed)
    else:
        _staggered(load, matmuls, finish)


def _ffn_call(x, ln_g, ln_b, alpha, li, hi, tm, weights, casts, p=None):
    n, d = x.shape
    tiles = n // tm
    cur, prev = _staggered_maps(tiles)
    main_specs = [pl.BlockSpec((tm, d), cur), pl.BlockSpec((tm, d), prev),
                  _resident(ln_g.shape), _resident(ln_b.shape)]
    main_args = [x, x, ln_g, ln_b]
    if p is not None:
        main_specs.append(pl.BlockSpec((None, tm, p.shape[-1]), lambda i: (li,) + prev(i)))
        main_args.append(p)
    body = functools.partial(_ffn_kernel, alpha, li, hi, p is not None, weights, casts)
    scratch = [pltpu.VMEM((tm, d), F32)]
    return _call(body, "ffn_ln" if p is None else "ffn_ln_ple", tiles, tm, d,
                 main_specs, main_args, weights, casts, scratch)


def _mixer_ab_kernel(alpha, li, ji, tm, tiles_per_seq, weights, casts, *refs):
    main, (win_v, wout_v), load_weights, o_ref, scratch = _split_refs(refs, 12, weights, casts)
    (sinks_ref, x_ref, xprev_ref, pos_ref, inv_ref, sgn_ref, sg_ref, sb_ref, ws_ref, bs_ref,
     lng_ref, lnb_ref) = main
    q_scr, kv_scr, u_scr, sv_scr, wcat_scr, cat_scr = scratch
    i = pl.program_id(0)
    last = pl.num_programs(0) - 1
    lane = lax.broadcasted_iota(jnp.int32, (1, LANES), 1)
    low_half = lane < HEAD_DIM
    slot = i % 2

    def load():
        load_weights()
        r = lax.broadcasted_iota(jnp.int32, (CHUNK, CHUNK), 0)
        c = lax.broadcasted_iota(jnp.int32, (CHUNK, CHUNK), 1)
        tril = r >= c
        for j in range(SGU_GROUPS // 2):
            a = jnp.where(tril, ws_ref[ji, 2 * j], 0.0)
            bb = jnp.where(tril, ws_ref[ji, 2 * j + 1], 0.0)
            wcat_scr[j] = jnp.concatenate([a, bb], axis=1).astype(BF16)

    def project():
        _mixer_ab_project(ji, tm, (i % tiles_per_seq) == 0, low_half, lane, x_ref, pos_ref,
                          win_v, inv_ref, sgn_ref, sg_ref, sb_ref,
                          q_scr.at[slot], kv_scr.at[slot], kv_scr.at[1 - slot],
                          u_scr.at[slot], sv_scr.at[slot])

    def attend():
        _mixer_ab_attend(ji, tm, ((i - 1) % tiles_per_seq) == 0, low_half, sinks_ref,
                         bs_ref, q_scr.at[1 - slot], kv_scr.at[1 - slot],
                         u_scr.at[1 - slot], sv_scr.at[1 - slot], wcat_scr, cat_scr)
        part = tm // OUT_PARTS
        for k in range(OUT_PARTS):
            rows = pl.ds(k * part, part)
            mix = _dot(cat_scr[rows, :], wout_v[...])
            o_ref[rows, :] = _layer_norm(alpha * xprev_ref[rows, :] + mix,
                                         lng_ref[li, 1:2, :], lnb_ref[li, 1:2, :])

    @pl.when(i == 0)
    def _first():
        load()
        project()

    @pl.when(jnp.logical_and(i > 0, i < last))
    def _steady():
        attend()
        project()

    @pl.when(i == last)
    def _last():
        attend()


def _mixer_ab_project(ji, tm, first_tile, low_half, lane, x_ref, pos_ref, win_v,
                      inv_ref, sgn_ref, sg_ref, sb_ref, q_scr, kv_scr, kv_other, u_scr, sv_scr):
    x = x_ref[...]
    h = _dot(x.astype(BF16), win_v[...])

    nseg = LANES // ROT_DIM
    seg = tm // nseg
    pos = pos_ref[...].astype(F32)
    packed = jnp.zeros((seg, LANES), F32)
    for s in range(nseg):
        in_seg = (lane // ROT_DIM) == s
        packed = jnp.where(in_seg, pos[:, s:s + 1], packed)
    ang = packed * inv_ref[...]
    cos_p = jnp.cos(ang)
    sin_p = jnp.sin(ang) * sgn_ref[...]
    rot_lo = lane < ROT_DIM
    rot_hi = jnp.logical_and(lane >= HEAD_DIM, lane < HEAD_DIM + ROT_DIM)

    def unpack(t, fill):
        parts = []
        for s in range(nseg):
            lo = t if s == 0 else pltpu.roll(t, LANES - s * ROT_DIM, axis=1)
            shift = (HEAD_DIM - s * ROT_DIM) % LANES
            hi = t if shift == 0 else pltpu.roll(t, shift, axis=1)
            parts.append(jnp.where(rot_lo, lo, jnp.where(rot_hi, hi, fill)))
        return jnp.concatenate(parts, axis=0)

    cos_t = unpack(cos_p, 1.0)
    sin_t = unpack(sin_p, 0.0)
    take_up = (lane % HEAD_DIM) < (ROT_DIM // 2)

    def rotary(t):
        up = pltpu.roll(t, LANES - ROT_DIM // 2, axis=1)
        dn = pltpu.roll(t, ROT_DIM // 2, axis=1)
        return t * cos_t + jnp.where(take_up, up, dn) * sin_t

    scale = HEAD_DIM ** -0.5
    for j in range(Q_W // LANES):
        qj = rotary(h[:, j * LANES:(j + 1) * LANES]) * scale
        q_scr[:, j * LANES:(j + 1) * LANES] = qj.astype(BF16)

    def dup_heads(t):
        sw = pltpu.roll(t, HEAD_DIM, axis=1)
        return jnp.where(low_half, t, sw), jnp.where(low_half, sw, t)

    k0, k1 = dup_heads(rotary(h[:, Q_W:Q_W + KV_W]))
    v0, v1 = dup_heads(h[:, Q_W + KV_W:Q_W + 2 * KV_W])
    kv_scr[WINDOW:, 0 * LANES:1 * LANES] = k0.astype(BF16)
    kv_scr[WINDOW:, 1 * LANES:2 * LANES] = k1.astype(BF16)
    kv_scr[WINDOW:, 2 * LANES:3 * LANES] = v0.astype(BF16)
    kv_scr[WINDOW:, 3 * LANES:4 * LANES] = v1.astype(BF16)

    su0 = Q_W + 2 * KV_W
    u_scr[...] = _gelu_tanh(h[:, su0:su0 + SGU_W])
    sv = _layer_norm(_gelu_tanh(h[:, su0 + SGU_W:su0 + 2 * SGU_W]),
                     sg_ref[ji:ji + 1, :], sb_ref[ji:ji + 1, :])
    sv_scr[...] = sv.astype(BF16)

    halo = kv_other[tm:tm + WINDOW, :]
    kv_scr[0:WINDOW, :] = jnp.where(first_tile, jnp.zeros_like(halo), halo)


def _mixer_ab_attend(ji, tm, first_tile, low_half, sinks_ref, bs_ref,
                     q_scr, kv_scr, u_scr, sv_scr, wcat_scr, cat_scr):
    nblk = tm // WINDOW
    qi = lax.broadcasted_iota(jnp.int32, (2 * WINDOW, 2 * WINDOW), 0) % WINDOW
    kj = lax.broadcasted_iota(jnp.int32, (2 * WINDOW, 2 * WINDOW), 1)
    upper_ok = kj <= qi + WINDOW
    row_first = lax.broadcasted_iota(jnp.int32, (2 * WINDOW, 1), 0) < WINDOW

    band_ok = jnp.logical_and(upper_ok, kj > qi)
    band_ok_first = jnp.logical_and(upper_ok, kj >= jnp.where(first_tile, WINDOW, qi + 1))

    def block(bi):
        r0 = bi * WINDOW
        valid = band_ok_first if bi == 0 else band_ok
        for kvh in range(ATT_KV_HEADS):
            kband = kv_scr[pl.ds(r0, 2 * WINDOW), kvh * LANES:(kvh + 1) * LANES]
            vband = kv_scr[pl.ds(r0, 2 * WINDOW), (2 + kvh) * LANES:(3 + kvh) * LANES]
            for pr in range(2):
                slab = kvh * 2 + pr
                qp = q_scr[pl.ds(r0, WINDOW), slab * LANES:(slab + 1) * LANES]
                blank = jnp.zeros_like(qp)
                qs = jnp.concatenate([jnp.where(low_half, qp, blank),
                                      jnp.where(low_half, blank, qp)], axis=0)
                s = lax.dot_general(qs, kband, (((1,), (1,)), ((), ())),
                                    preferred_element_type=F32)
                s = jnp.where(valid, s, NEG_INF)
                sink = jnp.where(row_first, sinks_ref[ji, 2 * slab], sinks_ref[ji, 2 * slab + 1])
                m = jnp.max(s, axis=-1, keepdims=True)
                p = jnp.exp(s - m)
                denom = jnp.sum(p, axis=-1, keepdims=True) + jnp.exp(sink - m)
                o = _dot(p.astype(BF16), vband) / denom
                att = jnp.where(low_half, o[:WINDOW], o[WINDOW:])
                cat_scr[pl.ds(r0, WINDOW), slab * LANES:(slab + 1) * LANES] = att.astype(BF16)
        for j in range(SGU_GROUPS // 2):
            vp = sv_scr[pl.ds(r0, CHUNK), j * LANES:(j + 1) * LANES]
            blank = jnp.zeros_like(vp)
            rhs = jnp.concatenate([jnp.where(low_half, vp, blank),
                                   jnp.where(low_half, blank, vp)], axis=0)
            mixed = _dot(wcat_scr[j], rhs) + bs_ref[:, j * LANES:(j + 1) * LANES]
            out = u_scr[pl.ds(r0, CHUNK), j * LANES:(j + 1) * LANES] * mixed
            cat_scr[pl.ds(r0, CHUNK), Q_W + j * LANES:Q_W + (j + 1) * LANES] = out.astype(BF16)

    for bi in range(nblk):
        block(bi)


def _mixer_ab_call(x, pos, sinks, inv_lane, sgn_lane, sgu_g, sgu_b, w_s, bias_t, ln_g, ln_b,
                   alpha, li, ji, tm, seq, weights, casts):
    n, d = x.shape
    tiles = n // tm
    body = functools.partial(_mixer_ab_kernel, alpha, li, ji, tm, seq // tm, weights, casts)
    cur, prev = _staggered_maps(tiles)
    main_specs = [
        pl.BlockSpec(memory_space=pltpu.SMEM),
        pl.BlockSpec((tm, d), cur),
        pl.BlockSpec((tm, d), prev),
        pl.BlockSpec((tm // pos.shape[1], pos.shape[1]), cur),
        _resident((1, LANES)), _resident((1, LANES)),
        _resident(sgu_g.shape), _resident(sgu_b.shape),
        _resident(w_s.shape),
        _resident((CHUNK, SGU_W)),
        _resident(ln_g.shape), _resident(ln_b.shape),
    ]
    main_args = [sinks, x, x, pos, inv_lane, sgn_lane, sgu_g, sgu_b, w_s, bias_t, ln_g, ln_b]
    scratch = [
        pltpu.VMEM((2, tm, Q_W), BF16),
        pltpu.VMEM((2, tm + WINDOW, 4 * LANES), BF16),
        pltpu.VMEM((2, tm, SGU_W), F32),
        pltpu.VMEM((2, tm, SGU_W), BF16),
        pltpu.VMEM((SGU_GROUPS // 2, CHUNK, 2 * CHUNK), BF16),
        pltpu.VMEM((tm, Q_W + SGU_W), BF16),
    ]
    return _call(body, "mixer_attn_sgu", tiles, tm, d, main_specs, main_args, weights, casts,
                 scratch)


def _mixer_conv_kernel(alpha, li, ji, tm, tiles_per_seq, weights, casts, *refs):
    main, (win_v, wout_v), load, o_ref, scratch = _split_refs(refs, 5, weights, casts)
    x_ref, xprev_ref, cw_ref, lng_ref, lnb_ref = main
    cz_scr, mix_keep = scratch
    i = pl.program_id(0)
    d = x_ref.shape[1]
    pad = SUBLANES

    @pl.when((i % tiles_per_seq) == 0)
    def _reset_halo():
        cz_scr[0:pad, :] = jnp.zeros((pad, d), F32)

    def matmuls(zero):
        h = _dot(x_ref[...].astype(BF16), win_v[...])
        cz = h[:, d:2 * d] * h[:, 2 * d:3 * d]
        cz_scr[pad:, :] = cz
        y = (cw_ref[ji, CONV_WIDTH - 1:CONV_WIDTH, :] + zero) * cz
        for t in range(CONV_WIDTH - 1):
            back = CONV_WIDTH - 1 - t
            y = y + cw_ref[ji, t:t + 1, :] * cz_scr[pad - back:pad - back + tm, :]
        cz_scr[0:pad, :] = cz_scr[tm:tm + pad, :]
        mix_keep[...] = _dot((h[:, 0:d] * y).astype(BF16), wout_v[...])

    def finish(chained):
        def store(rows, y):
            o_ref[rows, :] = y
        return _residual_ln(alpha, 1.0, xprev_ref, mix_keep,
                            lng_ref[li, 1:2, :], lnb_ref[li, 1:2, :], store, chained)

    _staggered(load, matmuls, finish)


def _mixer_conv_call(x, conv_w, ln_g, ln_b, alpha, li, ji, tm, seq, weights, casts):
    n, d = x.shape
    tiles = n // tm
    body = functools.partial(_mixer_conv_kernel, alpha, li, ji, tm, seq // tm, weights, casts)
    cur, prev = _staggered_maps(tiles)
    main_specs = [pl.BlockSpec((tm, d), cur), pl.BlockSpec((tm, d), prev),
                  _resident(conv_w.shape), _resident(ln_g.shape), _resident(ln_b.shape)]
    scratch = [pltpu.VMEM((tm + SUBLANES, d), F32),
               pltpu.VMEM((tm, d), F32)]
    return _call(body, "mixer_conv", tiles, tm, d, main_specs, [x, x, conv_w, ln_g, ln_b],
                 weights, casts, scratch)


def kernel(x, p, positions, ln_g, ln_b, ffn_w_gate, ffn_w_up, ffn_w_down, ab_w_in, ab_sinks,
           sgu_ln_g, sgu_ln_b, sgu_w_s, sgu_b_s, ab_w_out, sc_w_in, sc_conv_w, sc_w_out,
           ple_w_proj, ple_w_gate):
    bsz, seq, d = x.shape
    depth = p.shape[0]
    alpha = (2 * depth) ** 0.25
    tm = TM
    assert seq % tm == 0 and tm % WINDOW == 0
    assert sc_conv_w.shape[1] == CONV_WIDTH

    n = bsz * seq
    steps = n // tm + 1
    xs = x.reshape(n, d)
    nseg = LANES // ROT_DIM
    pos = positions.reshape(n // tm, nseg, tm // nseg).transpose(0, 2, 1).reshape(n // nseg, nseg)
    ps = p.reshape(depth, n, p.shape[-1])

    half = ROT_DIM // 2
    r = jnp.arange(LANES) % ROT_DIM
    inv_lane = jnp.power(ROPE_THETA, -(r % half).astype(F32) * (2.0 / ROT_DIM)).reshape(1, LANES)
    sgn_lane = jnp.where(r < half, -1.0, 1.0).reshape(1, LANES).astype(F32)

    calls = []
    for i in range(depth):
        j = i // 2
        ffn = lambda h, i=i: [(ffn_w_gate, (i, h)), (ffn_w_up, (i, h)), (ffn_w_down, (i, h))]
        calls.append(("ffn", i, 0, ffn(0)))
        if i % 2 == 0:
            calls.append(("attn", i, j, [(ab_w_in, (j,)), (ab_w_out, (j,))]))
        else:
            calls.append(("conv", i, j, [(sc_w_in, (j,)), (sc_w_out, (j,))]))
        calls.append(("ple", i, 1, ffn(1) + [(ple_w_proj, (i,)), (ple_w_gate, (i,))]))

    given = None
    for k, (kind, i, j, mats) in enumerate(calls):
        weights = _Weights(mats, given)
        casts = _Casts(calls[k + 1][3] if k + 1 < len(calls) else [], steps)
        if kind == "ffn":
            xs, given = _ffn_call(xs, ln_g, ln_b, alpha, i, j, tm, weights, casts)
        elif kind == "ple":
            xs, given = _ffn_call(xs, ln_g, ln_b, alpha, i, j, tm, weights, casts, p=ps)
        elif kind == "attn":
            bias_t = jnp.repeat(jnp.transpose(sgu_b_s[j]), SGU_GROUP_DIM, axis=1)
            xs, given = _mixer_ab_call(xs, pos, ab_sinks, inv_lane, sgn_lane, sgu_ln_g, sgu_ln_b,
                                       sgu_w_s, bias_t, ln_g, ln_b, alpha, i, j, tm, seq,
                                       weights, casts)
        else:
            xs, given = _mixer_conv_call(xs, sc_conv_w, ln_g, ln_b, alpha, i, j, tm, seq,
                                         weights, casts)
    return xs.reshape(bsz, seq, d)
```

```python
import functools
import math

import jax
import jax.numpy as jnp
from jax import lax
from jax.experimental import pallas as pl
from jax.experimental.pallas import tpu as pltpu

ATT_HEADS = 8
ATT_KV_HEADS = 2
HEAD_DIM = 64
WINDOW = 128
ROT_DIM = HEAD_DIM // 4
ROPE_THETA = 500000.0
SGU_GROUPS = 8
SGU_GROUP_DIM = 64
CHUNK = 128
CONV_WIDTH = 3
LN_EPS = 1e-5
NEG_INF = -1e30

Q_W = ATT_HEADS * HEAD_DIM
KV_W = ATT_KV_HEADS * HEAD_DIM
SGU_W = SGU_GROUPS * SGU_GROUP_DIM

LANES = 128
SUBLANES = 8
BF16_ROWS = 16
VMEM_LIMIT = 56 * 1024 * 1024
STAGE_BYTES = 3 * 512 * 1024
STAGE_SLOTS = 6
TM = 512
EMBED_PARTS = 2
OUT_PARTS = 2
LN_ROWS = 8
LN_AHEAD = 6

BF16 = jnp.bfloat16
F32 = jnp.float32


def _dot(a, b):
    return jnp.dot(a, b, preferred_element_type=F32)


def _layer_norm(y, g, b):
    mu = jnp.mean(y, axis=-1, keepdims=True)
    d = y - mu
    var = jnp.mean(d * d, axis=-1, keepdims=True)
    return d * lax.rsqrt(var + LN_EPS) * g + b


def _gelu_tanh(x):
    c = math.sqrt(2.0 / math.pi)
    half_x = 0.5 * x
    return half_x * jnp.tanh(x * ((c * 0.044715) * (x * x) + c)) + half_x


def _resident(shape):
    nd = len(shape)
    return pl.BlockSpec(shape, lambda i: (0,) * nd, pipeline_mode=pl.Buffered(1))


_HBM = pl.BlockSpec(memory_space=pl.ANY)


def _params():
    return pltpu.CompilerParams(dimension_semantics=("arbitrary",),
                                vmem_limit_bytes=VMEM_LIMIT)


def _stage(cols):
    rows = 1 << ((STAGE_BYTES // (4 * cols)).bit_length() - 1)
    return [pltpu.VMEM((STAGE_SLOTS, rows, cols), F32), pltpu.SemaphoreType.DMA((STAGE_SLOTS,))]


def _fetch_cast(jobs):
    chunks = []
    used = {}
    for src, dst, stage, sem in jobs:
        total, cols = src.shape
        rows = min(stage.shape[1], total)
        assert total % rows == 0 and cols == stage.shape[2] and dst.shape == src.shape
        for c in range(total // rows):
            slot = used.get(id(stage), 0) % STAGE_SLOTS
            used[id(stage)] = used.get(id(stage), 0) + 1
            staged = stage.at[slot, pl.ds(0, rows), :]
            copy = pltpu.make_async_copy(src.at[pl.ds(c * rows, rows), :], staged, sem.at[slot])
            chunks.append((copy, staged, dst.at[pl.ds(c * rows, rows), :]))
    for copy, _, _ in chunks[:STAGE_SLOTS]:
        copy.start()
    for n, (copy, staged, out) in enumerate(chunks):
        copy.wait()
        out[...] = staged[...].astype(BF16)
        if n + STAGE_SLOTS < len(chunks):
            chunks[n + STAGE_SLOTS][0].start()


class _Weights:
    def __init__(self, mats, given):
        self.mats, self.given = mats, given
        self.shapes = [tuple(a.shape[-2:]) for a, _ in mats]
        self.widths = sorted({s[1] for s in self.shapes})

    @property
    def args(self):
        return list(self.given) if self.given else [a for a, _ in self.mats]

    @property
    def in_specs(self):
        return [_resident(s) for s in self.shapes] if self.given else [_HBM] * len(self.mats)

    @property
    def scratch(self):
        if self.given:
            return []
        out = [pltpu.VMEM(s, BF16) for s in self.shapes]
        for w in self.widths:
            out += _stage(w)
        return out

    def bind(self, in_refs, scratch_refs):
        if self.given:
            return list(in_refs), (lambda: None)
        n = len(self.mats)
        stages = {w: (scratch_refs[n + 2 * k], scratch_refs[n + 2 * k + 1])
                  for k, w in enumerate(self.widths)}
        jobs = [(src.at[lead], dst) + stages[shape[1]]
                for src, (_, lead), dst, shape in zip(in_refs, self.mats, scratch_refs, self.shapes)]
        return list(scratch_refs[:n]), (lambda: _fetch_cast(jobs))


class _Casts:
    def __init__(self, mats, steps):
        self.mats = mats
        self.blocks = []
        for a, _ in mats:
            rows, cols = a.shape[-2:]
            rb = next(r for r in range(BF16_ROWS, rows + 1, BF16_ROWS)
                      if rows % r == 0 and rows // r <= steps)
            self.blocks.append((rb, rows // rb, rows, cols))

    @property
    def args(self):
        return [a for a, _ in self.mats]

    @property
    def in_specs(self):
        return [pl.BlockSpec((None,) * len(lead) + (rb, cols),
                             lambda i, lead=lead, nb=nb: lead + (jnp.minimum(i, nb - 1), 0))
                for (_, lead), (rb, nb, _, cols) in zip(self.mats, self.blocks)]

    @property
    def out_specs(self):
        return [pl.BlockSpec((rb, cols), lambda i, nb=nb: (jnp.minimum(i, nb - 1), 0))
                for rb, nb, _, cols in self.blocks]

    @property
    def out_shapes(self):
        return [jax.ShapeDtypeStruct((rows, cols), BF16) for _, _, rows, cols in self.blocks]

    @staticmethod
    def run(in_refs, out_refs):
        for src, dst in zip(in_refs, out_refs):
            dst[...] = src[...].astype(BF16)


def _split_refs(refs, n_main, weights, casts):
    nw, nc = len(weights.mats), len(casts.mats)
    main, rest = refs[:n_main], refs[n_main:]
    w_in, c_in = rest[:nw], rest[nw:nw + nc]
    o_ref, c_out = rest[nw + nc], rest[nw + nc + 1:nw + 2 * nc + 1]
    scratch = rest[nw + 2 * nc + 1:]
    ns = len(weights.scratch)
    w, load = weights.bind(w_in, scratch[:ns])
    _Casts.run(c_in, c_out)
    return main, w, load, o_ref, scratch[ns:]


def _call(body, name, tiles, tm, d, main_specs, main_args, weights, casts, scratch):
    outs = pl.pallas_call(
        body,
        grid=(tiles + 1,),
        in_specs=main_specs + weights.in_specs + casts.in_specs,
        out_specs=[pl.BlockSpec((tm, d), _staggered_maps(tiles)[1])] + casts.out_specs,
        out_shape=[jax.ShapeDtypeStruct((tiles * tm, d), F32)] + casts.out_shapes,
        scratch_shapes=weights.scratch + scratch,
        compiler_params=_params(),
        name=name,
    )(*main_args, *weights.args, *casts.args)
    return outs[0], list(outs[1:])


def _fold_bits(y):
    bits = lax.bitcast_convert_type(y, jnp.int32)
    cols = bits[:, 0:LANES]
    for j in range(1, y.shape[1] // LANES):
        cols = cols | bits[:, j * LANES:(j + 1) * LANES]
    rows = cols[0:SUBLANES]
    for k in range(1, y.shape[0] // SUBLANES):
        rows = rows | cols[k * SUBLANES:(k + 1) * SUBLANES]
    return rows


def _zero_from(bits):
    top = jnp.max(bits, axis=(0, 1), keepdims=True)
    cleared = lax.shift_right_logical(lax.shift_right_logical(top, 16), 16)
    return lax.bitcast_convert_type(cleared, F32)


def _residual_ln(alpha, beta, x_keep, r_keep, g, b, store, chained):
    if not chained:
        beta_r = r_keep[...] if beta == 1.0 else beta * r_keep[...]
        store(slice(None), _layer_norm(alpha * x_keep[...] + beta_r, g, b))
        return None
    ngroups = x_keep.shape[0] // LN_ROWS
    width = x_keep.shape[1]
    folded = []
    for c in range(ngroups):
        rows = pl.ds(c * LN_ROWS, LN_ROWS)
        a = alpha
        if c >= LN_AHEAD:
            z = lax.shift_right_logical(lax.shift_right_logical(folded[c - LN_AHEAD], 16), 16)
            a = alpha + jnp.tile(lax.bitcast_convert_type(z, F32),
                                 (LN_ROWS // SUBLANES, width // LANES))
        r = r_keep[rows, :]
        y = _layer_norm(a * x_keep[rows, :] + (r if beta == 1.0 else beta * r), g, b)
        store(rows, y)
        folded.append(_fold_bits(y))
    bits = folded[0]
    for f in folded[1:]:
        bits = bits | f
    return bits


def _staggered_maps(tiles):
    return (lambda i: (jnp.minimum(i, tiles - 1), 0)), (lambda i: (jnp.maximum(i - 1, 0), 0))


def _staggered(load, matmuls, finish, after=lambda: None):
    i = pl.program_id(0)
    last = pl.num_programs(0) - 1

    @pl.when(i == 0)
    def _first():
        load()
        matmuls(0.0)

    @pl.when(jnp.logical_and(i > 0, i < last))
    def _steady():
        matmuls(_zero_from(finish(True)))
        after()

    @pl.when(i == last)
    def _last():
        finish(False)
        after()


def _ffn_matmuls(x_ref, wg_v, wu_v, wd_v, ff_keep, zero):
    xb = x_ref[...].astype(BF16)
    gate = _dot(xb, wg_v[...])
    up = _dot(xb, wu_v[...])
    one = 1.0 + zero
    h = (gate * (one / (one + jnp.exp(-gate))) * up).astype(BF16)
    ff_keep[...] = _dot(h, wd_v[...])


def _ffn_kernel(alpha, li, hi, ple, weights, casts, *refs):
    main, w, load, o_ref, (ff_keep,) = _split_refs(refs, 5 if ple else 4, weights, casts)
    x_ref, xprev_ref, lng_ref, lnb_ref = main[:4]
    s = 2 * hi

    def store(rows, y):
        o_ref[rows, :] = y

    def finish(chained):
        return _residual_ln(alpha, 0.5, xprev_ref, ff_keep,
                            lng_ref[li, s:s + 1, :], lnb_ref[li, s:s + 1, :], store, chained)

    def embed():
        part = o_ref.shape[0] // EMBED_PARTS
        for k in range(EMBED_PARTS):
            rows = pl.ds(k * part, part)
            y = o_ref[rows, :]
            emb = _dot(main[4][rows, :].astype(BF16), w[3][...])
            gate = jax.nn.sigmoid(_dot(y.astype(BF16), w[4][...]))
            o_ref[rows, :] = y + emb * gate

    matmuls = functools.partial(_ffn_matmuls, x_ref, w[0], w[1], w[2], ff_keep)
    if ple:
        _staggered(load, matmuls, finish, embed)
    else:
        _staggered(load, matmuls, finish)


def _ffn_call(x, ln_g, ln_b, alpha, li, hi, tm, weights, casts, p=None):
    n, d = x.shape
    tiles = n // tm
    cur, prev = _staggered_maps(tiles)
    main_specs = [pl.BlockSpec((tm, d), cur), pl.BlockSpec((tm, d), prev),
                  _resident(ln_g.shape), _resident(ln_b.shape)]
    main_args = [x, x, ln_g, ln_b]
    if p is not None:
        main_specs.append(pl.BlockSpec((None, tm, p.shape[-1]), lambda i: (li,) + prev(i)))
        main_args.append(p)
    body = functools.partial(_ffn_kernel, alpha, li, hi, p is not None, weights, casts)
    scratch = [pltpu.VMEM((tm, d), F32)]
    return _call(body, "ffn_ln" if p is None else "ffn_ln_ple", tiles, tm, d,
                 main_specs, main_args, weights, casts, scratch)


def _mixer_ab_kernel(alpha, li, ji, tm, tiles_per_seq, weights, casts, *refs):
    main, (win_v, wout_v), load_weights, o_ref, scratch = _split_refs(refs, 12, weights, casts)
    (sinks_ref, x_ref, xprev_ref, pos_ref, inv_ref, sgn_ref, sg_ref, sb_ref, ws_ref, bs_ref,
     lng_ref, lnb_ref) = main
    q_scr, kv_scr, u_scr, sv_scr, wcat_scr, cat_scr = scratch
    i = pl.program_id(0)
    last = pl.num_programs(0) - 1
    lane = lax.broadcasted_iota(jnp.int32, (1, LANES), 1)
    low_half = lane < HEAD_DIM
    slot = i % 2

    def load():
        load_weights()
        r = lax.broadcasted_iota(jnp.int32, (CHUNK, CHUNK), 0)
        c = lax.broadcasted_iota(jnp.int32, (CHUNK, CHUNK), 1)
        tril = r >= c
        for j in range(SGU_GROUPS // 2):
            a = jnp.where(tril, ws_ref[ji, 2 * j], 0.0)
            bb = jnp.where(tril, ws_ref[ji, 2 * j + 1], 0.0)
            wcat_scr[j] = jnp.concatenate([a, bb], axis=1).astype(BF16)

    def project():
        _mixer_ab_project(ji, tm, (i % tiles_per_seq) == 0, low_half, lane, x_ref, pos_ref,
                          win_v, inv_ref, sgn_ref, sg_ref, sb_ref,
                          q_scr.at[slot], kv_scr.at[slot], kv_scr.at[1 - slot],
                          u_scr.at[slot], sv_scr.at[slot])

    def attend():
        _mixer_ab_attend(ji, tm, ((i - 1) % tiles_per_seq) == 0, low_half, sinks_ref,
                         bs_ref, q_scr.at[1 - slot], kv_scr.at[1 - slot],
                         u_scr.at[1 - slot], sv_scr.at[1 - slot], wcat_scr, cat_scr)
        part = tm // OUT_PARTS
        for k in range(OUT_PARTS):
            rows = pl.ds(k * part, part)
            mix = _dot(cat_scr[rows, :], wout_v[...])
            o_ref[rows, :] = _layer_norm(alpha * xprev_ref[rows, :] + mix,
                                         lng_ref[li, 1:2, :], lnb_ref[li, 1:2, :])

    @pl.when(i == 0)
    def _first():
        load()
        project()

    @pl.when(jnp.logical_and(i > 0, i < last))
    def _steady():
        attend()
        project()

    @pl.when(i == last)
    def _last():
        attend()


def _mixer_ab_project(ji, tm, first_tile, low_half, lane, x_ref, pos_ref, win_v,
                      inv_ref, sgn_ref, sg_ref, sb_ref, q_scr, kv_scr, kv_other, u_scr, sv_scr):
    x = x_ref[...]
    h = _dot(x.astype(BF16), win_v[...])

    nseg = LANES // ROT_DIM
    seg = tm // nseg
    pos = pos_ref[...].astype(F32)
    packed = jnp.zeros((seg, LANES), F32)
    for s in range(nseg):
        in_seg = (lane // ROT_DIM) == s
        packed = jnp.where(in_seg, pos[:, s:s + 1], packed)
    ang = packed * inv_ref[...]
    cos_p = jnp.cos(ang)
    sin_p = jnp.sin(ang) * sgn_ref[...]
    rot_lo = lane < ROT_DIM
    rot_hi = jnp.logical_and(lane >= HEAD_DIM, lane < HEAD_DIM + ROT_DIM)

    def unpack(t, fill):
        parts = []
        for s in range(nseg):
            lo = t if s == 0 else pltpu.roll(t, LANES - s * ROT_DIM, axis=1)
            shift = (HEAD_DIM - s * ROT_DIM) % LANES
            hi = t if shift == 0 else pltpu.roll(t, shift, axis=1)
            parts.append(jnp.where(rot_lo, lo, jnp.where(rot_hi, hi, fill)))
        return jnp.concatenate(parts, axis=0)

    cos_t = unpack(cos_p, 1.0)
    sin_t = unpack(sin_p, 0.0)
    take_up = (lane % HEAD_DIM) < (ROT_DIM // 2)

    def rotary(t):
        up = pltpu.roll(t, LANES - ROT_DIM // 2, axis=1)
        dn = pltpu.roll(t, ROT_DIM // 2, axis=1)
        return t * cos_t + jnp.where(take_up, up, dn) * sin_t

    scale = HEAD_DIM ** -0.5
    for j in range(Q_W // LANES):
        qj = rotary(h[:, j * LANES:(j + 1) * LANES]) * scale
        q_scr[:, j * LANES:(j + 1) * LANES] = qj.astype(BF16)

    def dup_heads(t):
        sw = pltpu.roll(t, HEAD_DIM, axis=1)
        return jnp.where(low_half, t, sw), jnp.where(low_half, sw, t)

    k0, k1 = dup_heads(rotary(h[:, Q_W:Q_W + KV_W]))
    v0, v1 = dup_heads(h[:, Q_W + KV_W:Q_W + 2 * KV_W])
    kv_scr[WINDOW:, 0 * LANES:1 * LANES] = k0.astype(BF16)
    kv_scr[WINDOW:, 1 * LANES:2 * LANES] = k1.astype(BF16)
    kv_scr[WINDOW:, 2 * LANES:3 * LANES] = v0.astype(BF16)
    kv_scr[WINDOW:, 3 * LANES:4 * LANES] = v1.astype(BF16)

    su0 = Q_W + 2 * KV_W
    u_scr[...] = _gelu_tanh(h[:, su0:su0 + SGU_W])
    sv = _layer_norm(_gelu_tanh(h[:, su0 + SGU_W:su0 + 2 * SGU_W]),
                     sg_ref[ji:ji + 1, :], sb_ref[ji:ji + 1, :])
    sv_scr[...] = sv.astype(BF16)

    halo = kv_other[tm:tm + WINDOW, :]
    kv_scr[0:WINDOW, :] = jnp.where(first_tile, jnp.zeros_like(halo), halo)


def _mixer_ab_attend(ji, tm, first_tile, low_half, sinks_ref, bs_ref,
                     q_scr, kv_scr, u_scr, sv_scr, wcat_scr, cat_scr):
    nblk = tm // WINDOW
    qi = lax.broadcasted_iota(jnp.int32, (2 * WINDOW, 2 * WINDOW), 0) % WINDOW
    kj = lax.broadcasted_iota(jnp.int32, (2 * WINDOW, 2 * WINDOW), 1)
    upper_ok = kj <= qi + WINDOW
    row_first = lax.broadcasted_iota(jnp.int32, (2 * WINDOW, 1), 0) < WINDOW

    band_ok = jnp.logical_and(upper_ok, kj > qi)
    band_ok_first = jnp.logical_and(upper_ok, kj >= jnp.where(first_tile, WINDOW, qi + 1))

    def block(bi):
        r0 = bi * WINDOW
        valid = band_ok_first if bi == 0 else band_ok
        for kvh in range(ATT_KV_HEADS):
            kband = kv_scr[pl.ds(r0, 2 * WINDOW), kvh * LANES:(kvh + 1) * LANES]
            vband = kv_scr[pl.ds(r0, 2 * WINDOW), (2 + kvh) * LANES:(3 + kvh) * LANES]
            for pr in range(2):
                slab = kvh * 2 + pr
                qp = q_scr[pl.ds(r0, WINDOW), slab * LANES:(slab + 1) * LANES]
                blank = jnp.zeros_like(qp)
                qs = jnp.concatenate([jnp.where(low_half, qp, blank),
                                      jnp.where(low_half, blank, qp)], axis=0)
                s = lax.dot_general(qs, kband, (((1,), (1,)), ((), ())),
                                    preferred_element_type=F32)
                s = jnp.where(valid, s, NEG_INF)
                sink = jnp.where(row_first, sinks_ref[ji, 2 * slab], sinks_ref[ji, 2 * slab + 1])
                m = jnp.max(s, axis=-1, keepdims=True)
                p = jnp.exp(s - m)
                denom = jnp.sum(p, axis=-1, keepdims=True) + jnp.exp(sink - m)
                o = _dot(p.astype(BF16), vband) / denom
                att = jnp.where(low_half, o[:WINDOW], o[WINDOW:])
                cat_scr[pl.ds(r0, WINDOW), slab * LANES:(slab + 1) * LANES] = att.astype(BF16)
        for j in range(SGU_GROUPS // 2):
            vp = sv_scr[pl.ds(r0, CHUNK), j * LANES:(j + 1) * LANES]
            blank = jnp.zeros_like(vp)
            rhs = jnp.concatenate([jnp.where(low_half, vp, blank),
                                   jnp.where(low_half, blank, vp)], axis=0)
            mixed = _dot(wcat_scr[j], rhs) + bs_ref[:, j * LANES:(j + 1) * LANES]
            out = u_scr[pl.ds(r0, CHUNK), j * LANES:(j + 1) * LANES] * mixed
            cat_scr[pl.ds(r0, CHUNK), Q_W + j * LANES:Q_W + (j + 1) * LANES] = out.astype(BF16)

    for bi in range(nblk):
        block(bi)


def _mixer_ab_call(x, pos, sinks, inv_lane, sgn_lane, sgu_g, sgu_b, w_s, bias_t, ln_g, ln_b,
                   alpha, li, ji, tm, seq, weights, casts):
    n, d = x.shape
    tiles = n // tm
    body = functools.partial(_mixer_ab_kernel, alpha, li, ji, tm, seq // tm, weights, casts)
    cur, prev = _staggered_maps(tiles)
    main_specs = [
        pl.BlockSpec(memory_space=pltpu.SMEM),
        pl.BlockSpec((tm, d), cur),
        pl.BlockSpec((tm, d), prev),
        pl.BlockSpec((tm // pos.shape[1], pos.shape[1]), cur),
        _resident((1, LANES)), _resident((1, LANES)),
        _resident(sgu_g.shape), _resident(sgu_b.shape),
        _resident(w_s.shape),
        _resident((CHUNK, SGU_W)),
        _resident(ln_g.shape), _resident(ln_b.shape),
    ]
    main_args = [sinks, x, x, pos, inv_lane, sgn_lane, sgu_g, sgu_b, w_s, bias_t, ln_g, ln_b]
    scratch = [
        pltpu.VMEM((2, tm, Q_W), BF16),
        pltpu.VMEM((2, tm + WINDOW, 4 * LANES), BF16),
        pltpu.VMEM((2, tm, SGU_W), F32),
        pltpu.VMEM((2, tm, SGU_W), BF16),
        pltpu.VMEM((SGU_GROUPS // 2, CHUNK, 2 * CHUNK), BF16),
        pltpu.VMEM((tm, Q_W + SGU_W), BF16),
    ]
    return _call(body, "mixer_attn_sgu", tiles, tm, d, main_specs, main_args, weights, casts,
                 scratch)


def _mixer_conv_kernel(alpha, li, ji, tm, tiles_per_seq, weights, casts, *refs):
    main, (win_v, wout_v), load, o_ref, scratch = _split_refs(refs, 5, weights, casts)
    x_ref, xprev_ref, cw_ref, lng_ref, lnb_ref = main
    cz_scr, mix_keep = scratch
    i = pl.program_id(0)
    d = x_ref.shape[1]
    pad = SUBLANES

    @pl.when((i % tiles_per_seq) == 0)
    def _reset_halo():
        cz_scr[0:pad, :] = jnp.zeros((pad, d), F32)

    def matmuls(zero):
        h = _dot(x_ref[...].astype(BF16), win_v[...])
        cz = h[:, d:2 * d] * h[:, 2 * d:3 * d]
        left = cz_scr[...]
        row = lax.broadcasted_iota(jnp.int32, (tm, 1), 0)
        y = (cw_ref[ji, CONV_WIDTH - 1:CONV_WIDTH, :] + zero) * cz
        for t in range(CONV_WIDTH - 1):
            back = CONV_WIDTH - 1 - t
            shifted = pltpu.roll(cz, back, axis=0)
            for r in range(back):
                shifted = jnp.where(row == r, left[pad - back + r:pad - back + r + 1, :], shifted)
            y = y + cw_ref[ji, t:t + 1, :] * shifted
        cz_scr[...] = cz[tm - pad:, :]
        mix_keep[...] = _dot((h[:, 0:d] * y).astype(BF16), wout_v[...])

    def finish(chained):
        def store(rows, y):
            o_ref[rows, :] = y
        return _residual_ln(alpha, 1.0, xprev_ref, mix_keep,
                            lng_ref[li, 1:2, :], lnb_ref[li, 1:2, :], store, chained)

    _staggered(load, matmuls, finish)


def _mixer_conv_call(x, conv_w, ln_g, ln_b, alpha, li, ji, tm, seq, weights, casts):
    n, d = x.shape
    tiles = n // tm
    body = functools.partial(_mixer_conv_kernel, alpha, li, ji, tm, seq // tm, weights, casts)
    cur, prev = _staggered_maps(tiles)
    main_specs = [pl.BlockSpec((tm, d), cur), pl.BlockSpec((tm, d), prev),
                  _resident(conv_w.shape), _resident(ln_g.shape), _resident(ln_b.shape)]
    scratch = [pltpu.VMEM((SUBLANES, d), F32),
               pltpu.VMEM((tm, d), F32)]
    return _call(body, "mixer_conv", tiles, tm, d, main_specs, [x, x, conv_w, ln_g, ln_b],
                 weights, casts, scratch)


def kernel(x, p, positions, ln_g, ln_b, ffn_w_gate, ffn_w_up, ffn_w_down, ab_w_in, ab_sinks,
           sgu_ln_g, sgu_ln_b, sgu_w_s, sgu_b_s, ab_w_out, sc_w_in, sc_conv_w, sc_w_out,
           ple_w_proj, ple_w_gate):
    bsz, seq, d = x.shape
    depth = p.shape[0]
    alpha = (2 * depth) ** 0.25
    tm = TM
    assert seq % tm == 0 and tm % WINDOW == 0
    assert sc_conv_w.shape[1] == CONV_WIDTH

    n = bsz * seq
    steps = n // tm + 1
    xs = x.reshape(n, d)
    nseg = LANES // ROT_DIM
    pos = positions.reshape(n // tm, nseg, tm // nseg).transpose(0, 2, 1).reshape(n // nseg, nseg)
    ps = p.reshape(depth, n, p.shape[-1])

    half = ROT_DIM // 2
    r = jnp.arange(LANES) % ROT_DIM
    inv_lane = jnp.power(ROPE_THETA, -(r % half).astype(F32) * (2.0 / ROT_DIM)).reshape(1, LANES)
    sgn_lane = jnp.where(r < half, -1.0, 1.0).reshape(1, LANES).astype(F32)

    calls = []
    for i in range(depth):
        j = i // 2
        ffn = lambda h, i=i: [(ffn_w_gate, (i, h)), (ffn_w_up, (i, h)), (ffn_w_down, (i, h))]
        calls.append(("ffn", i, 0, ffn(0)))
        if i % 2 == 0:
            calls.append(("attn", i, j, [(ab_w_in, (j,)), (ab_w_out, (j,))]))
        else:
            calls.append(("conv", i, j, [(sc_w_in, (j,)), (sc_w_out, (j,))]))
        calls.append(("ple", i, 1, ffn(1) + [(ple_w_proj, (i,)), (ple_w_gate, (i,))]))

    given = None
    for k, (kind, i, j, mats) in enumerate(calls):
        weights = _Weights(mats, given)
        casts = _Casts(calls[k + 1][3] if k + 1 < len(calls) else [], steps)
        if kind == "ffn":
            xs, given = _ffn_call(xs, ln_g, ln_b, alpha, i, j, tm, weights, casts)
        elif kind == "ple":
            xs, given = _ffn_call(xs, ln_g, ln_b, alpha, i, j, tm, weights, casts, p=ps)
        elif kind == "attn":
            bias_t = jnp.repeat(jnp.transpose(sgu_b_s[j]), SGU_GROUP_DIM, axis=1)
            xs, given = _mixer_ab_call(xs, pos, ab_sinks, inv_lane, sgn_lane, sgu_ln_g, sgu_ln_b,
                                       sgu_w_s, bias_t, ln_g, ln_b, alpha, i, j, tm, seq,
                                       weights, casts)
        else:
            xs, given = _mixer_conv_call(xs, sc_conv_w, ln_g, ln_b, alpha, i, j, tm, seq,
                                         weights, casts)
    return xs.reshape(bsz, seq, d)
```

```python
import functools
import math

import jax
import jax.numpy as jnp
from jax import lax
from jax.experimental import pallas as pl
from jax.experimental.pallas import tpu as pltpu

ATT_HEADS = 8
ATT_KV_HEADS = 2
HEAD_DIM = 64
WINDOW = 128
ROT_DIM = HEAD_DIM // 4
ROPE_THETA = 500000.0
SGU_GROUPS = 8
SGU_GROUP_DIM = 64
CHUNK = 128
CONV_WIDTH = 3
LN_EPS = 1e-5
NEG_INF = -1e30

Q_W = ATT_HEADS * HEAD_DIM
KV_W = ATT_KV_HEADS * HEAD_DIM
SGU_W = SGU_GROUPS * SGU_GROUP_DIM

LANES = 128
SUBLANES = 8
BF16_ROWS = 16
VMEM_LIMIT = 56 * 1024 * 1024
STAGE_BYTES = 3 * 512 * 1024
STAGE_SLOTS = 6
TM = 512
EMBED_PARTS = 2
OUT_PARTS = 2
LN_ROWS = 8
LN_AHEAD = 6

BF16 = jnp.bfloat16
F32 = jnp.float32


def _dot(a, b):
    return jnp.dot(a, b, preferred_element_type=F32)


def _layer_norm(y, g, b):
    mu = jnp.mean(y, axis=-1, keepdims=True)
    d = y - mu
    var = jnp.mean(d * d, axis=-1, keepdims=True)
    return d * lax.rsqrt(var + LN_EPS) * g + b


def _gelu_tanh(x):
    c = math.sqrt(2.0 / math.pi)
    half_x = 0.5 * x
    return half_x * jnp.tanh(x * ((c * 0.044715) * (x * x) + c)) + half_x


def _resident(shape):
    nd = len(shape)
    return pl.BlockSpec(shape, lambda i: (0,) * nd, pipeline_mode=pl.Buffered(1))


_HBM = pl.BlockSpec(memory_space=pl.ANY)


def _params():
    return pltpu.CompilerParams(dimension_semantics=("arbitrary",),
                                vmem_limit_bytes=VMEM_LIMIT)


def _stage(cols):
    rows = 1 << ((STAGE_BYTES // (4 * cols)).bit_length() - 1)
    return [pltpu.VMEM((STAGE_SLOTS, rows, cols), F32), pltpu.SemaphoreType.DMA((STAGE_SLOTS,))]


def _fetch_cast(jobs):
    chunks = []
    used = {}
    for src, dst, stage, sem in jobs:
        total, cols = src.shape
        rows = min(stage.shape[1], total)
        assert total % rows == 0 and cols == stage.shape[2] and dst.shape == src.shape
        for c in range(total // rows):
            slot = used.get(id(stage), 0) % STAGE_SLOTS
            used[id(stage)] = used.get(id(stage), 0) + 1
            staged = stage.at[slot, pl.ds(0, rows), :]
            copy = pltpu.make_async_copy(src.at[pl.ds(c * rows, rows), :], staged, sem.at[slot])
            chunks.append((copy, staged, dst.at[pl.ds(c * rows, rows), :]))
    for copy, _, _ in chunks[:STAGE_SLOTS]:
        copy.start()
    for n, (copy, staged, out) in enumerate(chunks):
        copy.wait()
        out[...] = staged[...].astype(BF16)
        if n + STAGE_SLOTS < len(chunks):
            chunks[n + STAGE_SLOTS][0].start()


class _Weights:
    def __init__(self, mats, given):
        self.mats, self.given = mats, given
        self.shapes = [tuple(a.shape[-2:]) for a, _ in mats]
        self.widths = sorted({s[1] for s in self.shapes})

    @property
    def args(self):
        return list(self.given) if self.given else [a for a, _ in self.mats]

    @property
    def in_specs(self):
        return [_resident(s) for s in self.shapes] if self.given else [_HBM] * len(self.mats)

    @property
    def scratch(self):
        if self.given:
            return []
        out = [pltpu.VMEM(s, BF16) for s in self.shapes]
        for w in self.widths:
            out += _stage(w)
        return out

    def bind(self, in_refs, scratch_refs):
        if self.given:
            return list(in_refs), (lambda: None)
        n = len(self.mats)
        stages = {w: (scratch_refs[n + 2 * k], scratch_refs[n + 2 * k + 1])
                  for k, w in enumerate(self.widths)}
        jobs = [(src.at[lead], dst) + stages[shape[1]]
                for src, (_, lead), dst, shape in zip(in_refs, self.mats, scratch_refs, self.shapes)]
        return list(scratch_refs[:n]), (lambda: _fetch_cast(jobs))


class _Casts:
    def __init__(self, mats, steps):
        self.mats = mats
        self.blocks = []
        for a, _ in mats:
            rows, cols = a.shape[-2:]
            rb = next(r for r in range(BF16_ROWS, rows + 1, BF16_ROWS)
                      if rows % r == 0 and rows // r <= steps)
            self.blocks.append((rb, rows // rb, rows, cols))

    @property
    def args(self):
        return [a for a, _ in self.mats]

    @property
    def in_specs(self):
        return [pl.BlockSpec((None,) * len(lead) + (rb, cols),
                             lambda i, lead=lead, nb=nb: lead + (jnp.minimum(i, nb - 1), 0))
                for (_, lead), (rb, nb, _, cols) in zip(self.mats, self.blocks)]

    @property
    def out_specs(self):
        return [pl.BlockSpec((rb, cols), lambda i, nb=nb: (jnp.minimum(i, nb - 1), 0))
                for rb, nb, _, cols in self.blocks]

    @property
    def out_shapes(self):
        return [jax.ShapeDtypeStruct((rows, cols), BF16) for _, _, rows, cols in self.blocks]

    @staticmethod
    def run(in_refs, out_refs):
        for src, dst in zip(in_refs, out_refs):
            dst[...] = src[...].astype(BF16)


def _split_refs(refs, n_main, weights, casts):
    nw, nc = len(weights.mats), len(casts.mats)
    main, rest = refs[:n_main], refs[n_main:]
    w_in, c_in = rest[:nw], rest[nw:nw + nc]
    o_ref, c_out = rest[nw + nc], rest[nw + nc + 1:nw + 2 * nc + 1]
    scratch = rest[nw + 2 * nc + 1:]
    ns = len(weights.scratch)
    w, load = weights.bind(w_in, scratch[:ns])
    _Casts.run(c_in, c_out)
    return main, w, load, o_ref, scratch[ns:]


def _call(body, name, tiles, tm, d, main_specs, main_args, weights, casts, scratch):
    outs = pl.pallas_call(
        body,
        grid=(tiles + 1,),
        in_specs=main_specs + weights.in_specs + casts.in_specs,
        out_specs=[pl.BlockSpec((tm, d), _staggered_maps(tiles)[1])] + casts.out_specs,
        out_shape=[jax.ShapeDtypeStruct((tiles * tm, d), F32)] + casts.out_shapes,
        scratch_shapes=weights.scratch + scratch,
        compiler_params=_params(),
        name=name,
    )(*main_args, *weights.args, *casts.args)
    return outs[0], list(outs[1:])


def _fold_bits(y):
    bits = lax.bitcast_convert_type(y, jnp.int32)
    cols = bits[:, 0:LANES]
    for j in range(1, y.shape[1] // LANES):
        cols = cols | bits[:, j * LANES:(j + 1) * LANES]
    rows = cols[0:SUBLANES]
    for k in range(1, y.shape[0] // SUBLANES):
        rows = rows | cols[k * SUBLANES:(k + 1) * SUBLANES]
    return rows


def _zero_from(bits):
    top = jnp.max(bits, axis=(0, 1), keepdims=True)
    cleared = lax.shift_right_logical(lax.shift_right_logical(top, 16), 16)
    return lax.bitcast_convert_type(cleared, F32)


def _residual_ln(alpha, beta, x_keep, r_keep, g, b, store, chained):
    if not chained:
        beta_r = r_keep[...] if beta == 1.0 else beta * r_keep[...]
        store(slice(None), _layer_norm(alpha * x_keep[...] + beta_r, g, b))
        return None
    ngroups = x_keep.shape[0] // LN_ROWS
    width = x_keep.shape[1]
    folded = []
    for c in range(ngroups):
        rows = pl.ds(c * LN_ROWS, LN_ROWS)
        a = alpha
        if c >= LN_AHEAD:
            z = lax.shift_right_logical(lax.shift_right_logical(folded[c - LN_AHEAD], 16), 16)
            a = alpha + jnp.tile(lax.bitcast_convert_type(z, F32),
                                 (LN_ROWS // SUBLANES, width // LANES))
        r = r_keep[rows, :]
        y = _layer_norm(a * x_keep[rows, :] + (r if beta == 1.0 else beta * r), g, b)
        store(rows, y)
        folded.append(_fold_bits(y))
    bits = folded[0]
    for f in folded[1:]:
        bits = bits | f
    return bits


def _staggered_maps(tiles):
    return (lambda i: (jnp.minimum(i, tiles - 1), 0)), (lambda i: (jnp.maximum(i - 1, 0), 0))


def _staggered(load, matmuls, finish, after=lambda: None):
    i = pl.program_id(0)
    last = pl.num_programs(0) - 1

    @pl.when(i == 0)
    def _first():
        load()
        matmuls(0.0)

    @pl.when(jnp.logical_and(i > 0, i < last))
    def _steady():
        matmuls(_zero_from(finish(True)))
        after()

    @pl.when(i == last)
    def _last():
        finish(False)
        after()


def _ffn_matmuls(x_ref, wg_v, wu_v, wd_v, ff_keep, zero):
    xb = x_ref[...].astype(BF16)
    gate = _dot(xb, wg_v[...])
    up = _dot(xb, wu_v[...])
    one = 1.0 + zero
    h = (gate * (one / (one + jnp.exp(-gate))) * up).astype(BF16)
    ff_keep[...] = _dot(h, wd_v[...])


def _ffn_kernel(alpha, li, hi, ple, weights, casts, *refs):
    main, w, load, o_ref, (ff_keep,) = _split_refs(refs, 5 if ple else 4, weights, casts)
    x_ref, xprev_ref, lng_ref, lnb_ref = main[:4]
    s = 2 * hi

    def store(rows, y):
        o_ref[rows, :] = y

    def finish(chained):
        return _residual_ln(alpha, 0.5, xprev_ref, ff_keep,
                            lng_ref[li, s:s + 1, :], lnb_ref[li, s:s + 1, :], store, chained)

    def embed():
        part = o_ref.shape[0] // EMBED_PARTS
        for k in range(EMBED_PARTS):
            rows = pl.ds(k * part, part)
            y = o_ref[rows, :]
            emb = _dot(main[4][rows, :].astype(BF16), w[3][...])
            gate = jax.nn.sigmoid(_dot(y.astype(BF16), w[4][...]))
            o_ref[rows, :] = y + emb * gate

    matmuls = functools.partial(_ffn_matmuls, x_ref, w[0], w[1], w[2], ff_keep)
    if ple:
        _staggered(load, matmuls, finish, embed)
    else:
        _staggered(load, matmuls, finish)


def _ffn_call(x, ln_g, ln_b, alpha, li, hi, tm, weights, casts, p=None):
    n, d = x.shape
    tiles = n // tm
    cur, prev = _staggered_maps(tiles)
    main_specs = [pl.BlockSpec((tm, d), cur), pl.BlockSpec((tm, d), prev),
                  _resident(ln_g.shape), _resident(ln_b.shape)]
    main_args = [x, x, ln_g, ln_b]
    if p is not None:
        main_specs.append(pl.BlockSpec((None, tm, p.shape[-1]), lambda i: (li,) + prev(i)))
        main_args.append(p)
    body = functools.partial(_ffn_kernel, alpha, li, hi, p is not None, weights, casts)
    scratch = [pltpu.VMEM((tm, d), F32)]
    return _call(body, "ffn_ln" if p is None else "ffn_ln_ple", tiles, tm, d,
                 main_specs, main_args, weights, casts, scratch)


def _mixer_ab_kernel(alpha, li, ji, tm, tiles_per_seq, weights, casts, *refs):
    main, (win_v, wout_v), load_weights, o_ref, scratch = _split_refs(refs, 12, weights, casts)
    (sinks_ref, x_ref, xprev_ref, pos_ref, inv_ref, sgn_ref, sg_ref, sb_ref, ws_ref, bs_ref,
     lng_ref, lnb_ref) = main
    q_scr, kv_scr, u_scr, sv_scr, wcat_scr, cat_scr = scratch
    i = pl.program_id(0)
    last = pl.num_programs(0) - 1
    lane = lax.broadcasted_iota(jnp.int32, (1, LANES), 1)
    low_half = lane < HEAD_DIM
    slot = i % 2

    def load():
        load_weights()
        r = lax.broadcasted_iota(jnp.int32, (CHUNK, CHUNK), 0)
        c = lax.broadcasted_iota(jnp.int32, (CHUNK, CHUNK), 1)
        tril = r >= c
        for j in range(SGU_GROUPS // 2):
            a = jnp.where(tril, ws_ref[ji, 2 * j], 0.0)
            bb = jnp.where(tril, ws_ref[ji, 2 * j + 1], 0.0)
            wcat_scr[j] = jnp.concatenate([a, bb], axis=1).astype(BF16)

    def project():
        _mixer_ab_project(ji, tm, (i % tiles_per_seq) == 0, low_half, lane, x_ref, pos_ref,
                          win_v, inv_ref, sgn_ref, sg_ref, sb_ref,
                          q_scr.at[slot], kv_scr.at[slot], kv_scr.at[1 - slot],
                          u_scr.at[slot], sv_scr.at[slot])

    def attend():
        _mixer_ab_attend(ji, tm, ((i - 1) % tiles_per_seq) == 0, low_half, sinks_ref,
                         bs_ref, q_scr.at[1 - slot], kv_scr.at[1 - slot],
                         u_scr.at[1 - slot], sv_scr.at[1 - slot], wcat_scr, cat_scr)
        part = tm // OUT_PARTS
        for k in range(OUT_PARTS):
            rows = pl.ds(k * part, part)
            mix = _dot(cat_scr[rows, :], wout_v[...])
            o_ref[rows, :] = _layer_norm(alpha * xprev_ref[rows, :] + mix,
                                         lng_ref[li, 1:2, :], lnb_ref[li, 1:2, :])

    @pl.when(i == 0)
    def _first():
        load()
        project()

    @pl.when(jnp.logical_and(i > 0, i < last))
    def _steady():
        attend()
        project()

    @pl.when(i == last)
    def _last():
        attend()


def _mixer_ab_project(ji, tm, first_tile, low_half, lane, x_ref, pos_ref, win_v,
                      inv_ref, sgn_ref, sg_ref, sb_ref, q_scr, kv_scr, kv_other, u_scr, sv_scr):
    x = x_ref[...]
    h = _dot(x.astype(BF16), win_v[...])

    nseg = LANES // ROT_DIM
    seg = tm // nseg
    pos = pos_ref[...].astype(F32)
    packed = jnp.zeros((seg, LANES), F32)
    for s in range(nseg):
        in_seg = (lane // ROT_DIM) == s
        packed = jnp.where(in_seg, pos[:, s:s + 1], packed)
    ang = packed * inv_ref[...]
    cos_p = jnp.cos(ang)
    sin_p = jnp.sin(ang) * sgn_ref[...]
    rot_lo = lane < ROT_DIM
    rot_hi = jnp.logical_and(lane >= HEAD_DIM, lane < HEAD_DIM + ROT_DIM)

    def unpack(t, fill):
        parts = []
        for s in range(nseg):
            lo = t if s == 0 else pltpu.roll(t, LANES - s * ROT_DIM, axis=1)
            shift = (HEAD_DIM - s * ROT_DIM) % LANES
            hi = t if shift == 0 else pltpu.roll(t, shift, axis=1)
            parts.append(jnp.where(rot_lo, lo, jnp.where(rot_hi, hi, fill)))
        return jnp.concatenate(parts, axis=0)

    cos_t = unpack(cos_p, 1.0)
    sin_t = unpack(sin_p, 0.0)
    take_up = (lane % HEAD_DIM) < (ROT_DIM // 2)

    def rotary(t):
        up = pltpu.roll(t, LANES - ROT_DIM // 2, axis=1)
        dn = pltpu.roll(t, ROT_DIM // 2, axis=1)
        return t * cos_t + jnp.where(take_up, up, dn) * sin_t

    scale = HEAD_DIM ** -0.5
    for j in range(Q_W // LANES):
        qj = rotary(h[:, j * LANES:(j + 1) * LANES]) * scale
        q_scr[:, j * LANES:(j + 1) * LANES] = qj.astype(BF16)

    def dup_heads(t):
        sw = pltpu.roll(t, HEAD_DIM, axis=1)
        return jnp.where(low_half, t, sw), jnp.where(low_half, sw, t)

    k0, k1 = dup_heads(rotary(h[:, Q_W:Q_W + KV_W]))
    v0, v1 = dup_heads(h[:, Q_W + KV_W:Q_W + 2 * KV_W])
    kv_scr[WINDOW:, 0 * LANES:1 * LANES] = k0.astype(BF16)
    kv_scr[WINDOW:, 1 * LANES:2 * LANES] = k1.astype(BF16)
    kv_scr[WINDOW:, 2 * LANES:3 * LANES] = v0.astype(BF16)
    kv_scr[WINDOW:, 3 * LANES:4 * LANES] = v1.astype(BF16)

    su0 = Q_W + 2 * KV_W
    u_scr[...] = _gelu_tanh(h[:, su0:su0 + SGU_W])
    sv = _layer_norm(_gelu_tanh(h[:, su0 + SGU_W:su0 + 2 * SGU_W]),
                     sg_ref[ji:ji + 1, :], sb_ref[ji:ji + 1, :])
    sv_scr[...] = sv.astype(BF16)

    halo = kv_other[tm:tm + WINDOW, :]
    kv_scr[0:WINDOW, :] = jnp.where(first_tile, jnp.zeros_like(halo), halo)


def _mixer_ab_attend(ji, tm, first_tile, low_half, sinks_ref, bs_ref,
                     q_scr, kv_scr, u_scr, sv_scr, wcat_scr, cat_scr):
    nblk = tm // WINDOW
    qi = lax.broadcasted_iota(jnp.int32, (2 * WINDOW, 2 * WINDOW), 0) % WINDOW
    kj = lax.broadcasted_iota(jnp.int32, (2 * WINDOW, 2 * WINDOW), 1)
    upper_ok = kj <= qi + WINDOW
    row_first = lax.broadcasted_iota(jnp.int32, (2 * WINDOW, 1), 0) < WINDOW

    band_ok = jnp.logical_and(upper_ok, kj > qi)
    band_ok_first = jnp.logical_and(upper_ok, kj >= jnp.where(first_tile, WINDOW, qi + 1))

    def block(bi):
        r0 = bi * WINDOW
        valid = band_ok_first if bi == 0 else band_ok
        for kvh in range(ATT_KV_HEADS):
            kband = kv_scr[pl.ds(r0, 2 * WINDOW), kvh * LANES:(kvh + 1) * LANES]
            vband = kv_scr[pl.ds(r0, 2 * WINDOW), (2 + kvh) * LANES:(3 + kvh) * LANES]
            for pr in range(2):
                slab = kvh * 2 + pr
                qp = q_scr[pl.ds(r0, WINDOW), slab * LANES:(slab + 1) * LANES]
                blank = jnp.zeros_like(qp)
                qs = jnp.concatenate([jnp.where(low_half, qp, blank),
                                      jnp.where(low_half, blank, qp)], axis=0)
                s = lax.dot_general(qs, kband, (((1,), (1,)), ((), ())),
                                    preferred_element_type=F32)
                s = jnp.where(valid, s, NEG_INF)
                sink = jnp.where(row_first, sinks_ref[ji, 2 * slab], sinks_ref[ji, 2 * slab + 1])
                m = jnp.max(s, axis=-1, keepdims=True)
                p = jnp.exp(s - m)
                ov = _dot(p.astype(BF16), jnp.concatenate([vband, jnp.ones_like(vband)], axis=1))
                o = ov[:, :LANES] / (ov[:, LANES:] + jnp.exp(sink - m))
                att = jnp.where(low_half, o[:WINDOW], o[WINDOW:])
                cat_scr[pl.ds(r0, WINDOW), slab * LANES:(slab + 1) * LANES] = att.astype(BF16)
        for j in range(SGU_GROUPS // 2):
            vp = sv_scr[pl.ds(r0, CHUNK), j * LANES:(j + 1) * LANES]
            blank = jnp.zeros_like(vp)
            rhs = jnp.concatenate([jnp.where(low_half, vp, blank),
                                   jnp.where(low_half, blank, vp)], axis=0)
            mixed = _dot(wcat_scr[j], rhs) + bs_ref[:, j * LANES:(j + 1) * LANES]
            out = u_scr[pl.ds(r0, CHUNK), j * LANES:(j + 1) * LANES] * mixed
            cat_scr[pl.ds(r0, CHUNK), Q_W + j * LANES:Q_W + (j + 1) * LANES] = out.astype(BF16)

    for bi in range(nblk):
        block(bi)


def _mixer_ab_call(x, pos, sinks, inv_lane, sgn_lane, sgu_g, sgu_b, w_s, bias_t, ln_g, ln_b,
                   alpha, li, ji, tm, seq, weights, casts):
    n, d = x.shape
    tiles = n // tm
    body = functools.partial(_mixer_ab_kernel, alpha, li, ji, tm, seq // tm, weights, casts)
    cur, prev = _staggered_maps(tiles)
    main_specs = [
        pl.BlockSpec(memory_space=pltpu.SMEM),
        pl.BlockSpec((tm, d), cur),
        pl.BlockSpec((tm, d), prev),
        pl.BlockSpec((tm // pos.shape[1], pos.shape[1]), cur),
        _resident((1, LANES)), _resident((1, LANES)),
        _resident(sgu_g.shape), _resident(sgu_b.shape),
        _resident(w_s.shape),
        _resident((CHUNK, SGU_W)),
        _resident(ln_g.shape), _resident(ln_b.shape),
    ]
    main_args = [sinks, x, x, pos, inv_lane, sgn_lane, sgu_g, sgu_b, w_s, bias_t, ln_g, ln_b]
    scratch = [
        pltpu.VMEM((2, tm, Q_W), BF16),
        pltpu.VMEM((2, tm + WINDOW, 4 * LANES), BF16),
        pltpu.VMEM((2, tm, SGU_W), F32),
        pltpu.VMEM((2, tm, SGU_W), BF16),
        pltpu.VMEM((SGU_GROUPS // 2, CHUNK, 2 * CHUNK), BF16),
        pltpu.VMEM((tm, Q_W + SGU_W), BF16),
    ]
    return _call(body, "mixer_attn_sgu", tiles, tm, d, main_specs, main_args, weights, casts,
                 scratch)


def _mixer_conv_kernel(alpha, li, ji, tm, tiles_per_seq, weights, casts, *refs):
    main, (win_v, wout_v), load, o_ref, scratch = _split_refs(refs, 5, weights, casts)
    x_ref, xprev_ref, cw_ref, lng_ref, lnb_ref = main
    cz_scr, mix_keep = scratch
    i = pl.program_id(0)
    d = x_ref.shape[1]
    pad = SUBLANES

    @pl.when((i % tiles_per_seq) == 0)
    def _reset_halo():
        cz_scr[0:pad, :] = jnp.zeros((pad, d), F32)

    def matmuls(zero):
        h = _dot(x_ref[...].astype(BF16), win_v[...])
        cz = h[:, d:2 * d] * h[:, 2 * d:3 * d]
        left = cz_scr[...]
        row = lax.broadcasted_iota(jnp.int32, (tm, 1), 0)
        y = (cw_ref[ji, CONV_WIDTH - 1:CONV_WIDTH, :] + zero) * cz
        for t in range(CONV_WIDTH - 1):
            back = CONV_WIDTH - 1 - t
            shifted = pltpu.roll(cz, back, axis=0)
            for r in range(back):
                shifted = jnp.where(row == r, left[pad - back + r:pad - back + r + 1, :], shifted)
            y = y + cw_ref[ji, t:t + 1, :] * shifted
        cz_scr[...] = cz[tm - pad:, :]
        mix_keep[...] = _dot((h[:, 0:d] * y).astype(BF16), wout_v[...])

    def finish(chained):
        def store(rows, y):
            o_ref[rows, :] = y
        return _residual_ln(alpha, 1.0, xprev_ref, mix_keep,
                            lng_ref[li, 1:2, :], lnb_ref[li, 1:2, :], store, chained)

    _staggered(load, matmuls, finish)


def _mixer_conv_call(x, conv_w, ln_g, ln_b, alpha, li, ji, tm, seq, weights, casts):
    n, d = x.shape
    tiles = n // tm
    body = functools.partial(_mixer_conv_kernel, alpha, li, ji, tm, seq // tm, weights, casts)
    cur, prev = _staggered_maps(tiles)
    main_specs = [pl.BlockSpec((tm, d), cur), pl.BlockSpec((tm, d), prev),
                  _resident(conv_w.shape), _resident(ln_g.shape), _resident(ln_b.shape)]
    scratch = [pltpu.VMEM((SUBLANES, d), F32),
               pltpu.VMEM((tm, d), F32)]
    return _call(body, "mixer_conv", tiles, tm, d, main_specs, [x, x, conv_w, ln_g, ln_b],
                 weights, casts, scratch)


def kernel(x, p, positions, ln_g, ln_b, ffn_w_gate, ffn_w_up, ffn_w_down, ab_w_in, ab_sinks,
           sgu_ln_g, sgu_ln_b, sgu_w_s, sgu_b_s, ab_w_out, sc_w_in, sc_conv_w, sc_w_out,
           ple_w_proj, ple_w_gate):
    bsz, seq, d = x.shape
    depth = p.shape[0]
    alpha = (2 * depth) ** 0.25
    tm = TM
    assert seq % tm == 0 and tm % WINDOW == 0
    assert sc_conv_w.shape[1] == CONV_WIDTH

    n = bsz * seq
    steps = n // tm + 1
    xs = x.reshape(n, d)
    nseg = LANES // ROT_DIM
    pos = positions.reshape(n // tm, nseg, tm // nseg).transpose(0, 2, 1).reshape(n // nseg, nseg)
    ps = p.reshape(depth, n, p.shape[-1])

    half = ROT_DIM // 2
    r = jnp.arange(LANES) % ROT_DIM
    inv_lane = jnp.power(ROPE_THETA, -(r % half).astype(F32) * (2.0 / ROT_DIM)).reshape(1, LANES)
    sgn_lane = jnp.where(r < half, -1.0, 1.0).reshape(1, LANES).astype(F32)

    calls = []
    for i in range(depth):
        j = i // 2
        ffn = lambda h, i=i: [(ffn_w_gate, (i, h)), (ffn_w_up, (i, h)), (ffn_w_down, (i, h))]
        calls.append(("ffn", i, 0, ffn(0)))
        if i % 2 == 0:
            calls.append(("attn", i, j, [(ab_w_in, (j,)), (ab_w_out, (j,))]))
        else:
            calls.append(("conv", i, j, [(sc_w_in, (j,)), (sc_w_out, (j,))]))
        calls.append(("ple", i, 1, ffn(1) + [(ple_w_proj, (i,)), (ple_w_gate, (i,))]))

    given = None
    for k, (kind, i, j, mats) in enumerate(calls):
        weights = _Weights(mats, given)
        casts = _Casts(calls[k + 1][3] if k + 1 < len(calls) else [], steps)
        if kind == "ffn":
            xs, given = _ffn_call(xs, ln_g, ln_b, alpha, i, j, tm, weights, casts)
        elif kind == "ple":
            xs, given = _ffn_call(xs, ln_g, ln_b, alpha, i, j, tm, weights, casts, p=ps)
        elif kind == "attn":
            bias_t = jnp.repeat(jnp.transpose(sgu_b_s[j]), SGU_GROUP_DIM, axis=1)
            xs, given = _mixer_ab_call(xs, pos, ab_sinks, inv_lane, sgn_lane, sgu_ln_g, sgu_ln_b,
                                       sgu_w_s, bias_t, ln_g, ln_b, alpha, i, j, tm, seq,
                                       weights, casts)
        else:
            xs, given = _mixer_conv_call(xs, sc_conv_w, ln_g, ln_b, alpha, i, j, tm, seq,
                                         weights, casts)
    return xs.reshape(bsz, seq, d)
```

```python
import functools
import math

import jax
import jax.numpy as jnp
from jax import lax
from jax.experimental import pallas as pl
from jax.experimental.pallas import tpu as pltpu

ATT_HEADS = 8
ATT_KV_HEADS = 2
HEAD_DIM = 64
WINDOW = 128
ROT_DIM = HEAD_DIM // 4
ROPE_THETA = 500000.0
SGU_GROUPS = 8
SGU_GROUP_DIM = 64
CHUNK = 128
CONV_WIDTH = 3
LN_EPS = 1e-5
NEG_INF = -1e30

Q_W = ATT_HEADS * HEAD_DIM
KV_W = ATT_KV_HEADS * HEAD_DIM
SGU_W = SGU_GROUPS * SGU_GROUP_DIM

LANES = 128
SUBLANES = 8
BF16_ROWS = 16
VMEM_LIMIT = 56 * 1024 * 1024
STAGE_BYTES = 3 * 512 * 1024
STAGE_SLOTS = 6
TM = 512
EMBED_PARTS = 2
OUT_PARTS = 2
LN_ROWS = 8
LN_AHEAD = 6

BF16 = jnp.bfloat16
F32 = jnp.float32


def _dot(a, b):
    return jnp.dot(a, b, preferred_element_type=F32)


def _layer_norm(y, g, b):
    mu = jnp.mean(y, axis=-1, keepdims=True)
    d = y - mu
    var = jnp.mean(d * d, axis=-1, keepdims=True)
    return d * lax.rsqrt(var + LN_EPS) * g + b


def _gelu_tanh(x):
    c = math.sqrt(2.0 / math.pi)
    half_x = 0.5 * x
    return half_x * jnp.tanh(x * ((c * 0.044715) * (x * x) + c)) + half_x


def _resident(shape):
    nd = len(shape)
    return pl.BlockSpec(shape, lambda i: (0,) * nd, pipeline_mode=pl.Buffered(1))


_HBM = pl.BlockSpec(memory_space=pl.ANY)


def _params():
    return pltpu.CompilerParams(dimension_semantics=("arbitrary",),
                                vmem_limit_bytes=VMEM_LIMIT)


def _stage(cols):
    rows = 1 << ((STAGE_BYTES // (4 * cols)).bit_length() - 1)
    return [pltpu.VMEM((STAGE_SLOTS, rows, cols), F32), pltpu.SemaphoreType.DMA((STAGE_SLOTS,))]


def _fetch_cast(jobs):
    chunks = []
    used = {}
    for src, dst, stage, sem in jobs:
        total, cols = src.shape
        rows = min(stage.shape[1], total)
        assert total % rows == 0 and cols == stage.shape[2] and dst.shape == src.shape
        for c in range(total // rows):
            slot = used.get(id(stage), 0) % STAGE_SLOTS
            used[id(stage)] = used.get(id(stage), 0) + 1
            staged = stage.at[slot, pl.ds(0, rows), :]
            copy = pltpu.make_async_copy(src.at[pl.ds(c * rows, rows), :], staged, sem.at[slot])
            chunks.append((copy, staged, dst.at[pl.ds(c * rows, rows), :]))
    for copy, _, _ in chunks[:STAGE_SLOTS]:
        copy.start()
    for n, (copy, staged, out) in enumerate(chunks):
        copy.wait()
        out[...] = staged[...].astype(BF16)
        if n + STAGE_SLOTS < len(chunks):
            chunks[n + STAGE_SLOTS][0].start()


class _Weights:
    def __init__(self, mats, given):
        self.mats, self.given = mats, given
        self.shapes = [tuple(a.shape[-2:]) for a, _ in mats]
        self.widths = sorted({s[1] for s in self.shapes})

    @property
    def args(self):
        return list(self.given) if self.given else [a for a, _ in self.mats]

    @property
    def in_specs(self):
        return [_resident(s) for s in self.shapes] if self.given else [_HBM] * len(self.mats)

    @property
    def scratch(self):
        if self.given:
            return []
        out = [pltpu.VMEM(s, BF16) for s in self.shapes]
        for w in self.widths:
            out += _stage(w)
        return out

    def bind(self, in_refs, scratch_refs):
        if self.given:
            return list(in_refs), (lambda: None)
        n = len(self.mats)
        stages = {w: (scratch_refs[n + 2 * k], scratch_refs[n + 2 * k + 1])
                  for k, w in enumerate(self.widths)}
        jobs = [(src.at[lead], dst) + stages[shape[1]]
                for src, (_, lead), dst, shape in zip(in_refs, self.mats, scratch_refs, self.shapes)]
        return list(scratch_refs[:n]), (lambda: _fetch_cast(jobs))


class _Casts:
    def __init__(self, mats, steps):
        self.mats = mats
        self.blocks = []
        for a, _ in mats:
            rows, cols = a.shape[-2:]
            rb = next(r for r in range(BF16_ROWS, rows + 1, BF16_ROWS)
                      if rows % r == 0 and rows // r <= steps)
            self.blocks.append((rb, rows // rb, rows, cols))

    @property
    def args(self):
        return [a for a, _ in self.mats]

    @property
    def in_specs(self):
        return [pl.BlockSpec((None,) * len(lead) + (rb, cols),
                             lambda i, lead=lead, nb=nb: lead + (jnp.minimum(i, nb - 1), 0))
                for (_, lead), (rb, nb, _, cols) in zip(self.mats, self.blocks)]

    @property
    def out_specs(self):
        return [pl.BlockSpec((rb, cols), lambda i, nb=nb: (jnp.minimum(i, nb - 1), 0))
                for rb, nb, _, cols in self.blocks]

    @property
    def out_shapes(self):
        return [jax.ShapeDtypeStruct((rows, cols), BF16) for _, _, rows, cols in self.blocks]

    @staticmethod
    def run(in_refs, out_refs):
        for src, dst in zip(in_refs, out_refs):
            dst[...] = src[...].astype(BF16)


def _split_refs(refs, n_main, weights, casts):
    nw, nc = len(weights.mats), len(casts.mats)
    main, rest = refs[:n_main], refs[n_main:]
    w_in, c_in = rest[:nw], rest[nw:nw + nc]
    o_ref, c_out = rest[nw + nc], rest[nw + nc + 1:nw + 2 * nc + 1]
    scratch = rest[nw + 2 * nc + 1:]
    ns = len(weights.scratch)
    w, load = weights.bind(w_in, scratch[:ns])
    _Casts.run(c_in, c_out)
    return main, w, load, o_ref, scratch[ns:]


def _call(body, name, tiles, tm, d, main_specs, main_args, weights, casts, scratch):
    outs = pl.pallas_call(
        body,
        grid=(tiles + 1,),
        in_specs=main_specs + weights.in_specs + casts.in_specs,
        out_specs=[pl.BlockSpec((tm, d), _staggered_maps(tiles)[1])] + casts.out_specs,
        out_shape=[jax.ShapeDtypeStruct((tiles * tm, d), F32)] + casts.out_shapes,
        scratch_shapes=weights.scratch + scratch,
        compiler_params=_params(),
        name=name,
    )(*main_args, *weights.args, *casts.args)
    return outs[0], list(outs[1:])


def _fold_bits(y):
    bits = lax.bitcast_convert_type(y, jnp.int32)
    cols = bits[:, 0:LANES]
    for j in range(1, y.shape[1] // LANES):
        cols = cols | bits[:, j * LANES:(j + 1) * LANES]
    rows = cols[0:SUBLANES]
    for k in range(1, y.shape[0] // SUBLANES):
        rows = rows | cols[k * SUBLANES:(k + 1) * SUBLANES]
    return rows


def _zero_from(bits):
    top = jnp.max(bits, axis=(0, 1), keepdims=True)
    cleared = lax.shift_right_logical(lax.shift_right_logical(top, 16), 16)
    return lax.bitcast_convert_type(cleared, F32)


def _residual_ln(alpha, beta, x_keep, r_keep, g, b, store, chained):
    if not chained:
        beta_r = r_keep[...] if beta == 1.0 else beta * r_keep[...]
        store(slice(None), _layer_norm(alpha * x_keep[...] + beta_r, g, b))
        return None
    ngroups = x_keep.shape[0] // LN_ROWS
    width = x_keep.shape[1]
    folded = []
    for c in range(ngroups):
        rows = pl.ds(c * LN_ROWS, LN_ROWS)
        a = alpha
        if c >= LN_AHEAD:
            z = lax.shift_right_logical(lax.shift_right_logical(folded[c - LN_AHEAD], 16), 16)
            a = alpha + jnp.tile(lax.bitcast_convert_type(z, F32),
                                 (LN_ROWS // SUBLANES, width // LANES))
        r = r_keep[rows, :]
        y = _layer_norm(a * x_keep[rows, :] + (r if beta == 1.0 else beta * r), g, b)
        store(rows, y)
        folded.append(_fold_bits(y))
    bits = folded[0]
    for f in folded[1:]:
        bits = bits | f
    return bits


def _staggered_maps(tiles):
    return (lambda i: (jnp.minimum(i, tiles - 1), 0)), (lambda i: (jnp.maximum(i - 1, 0), 0))


def _staggered(load, matmuls, finish, after=lambda: None):
    i = pl.program_id(0)
    last = pl.num_programs(0) - 1

    @pl.when(i == 0)
    def _first():
        load()
        matmuls(0.0)

    @pl.when(jnp.logical_and(i > 0, i < last))
    def _steady():
        matmuls(_zero_from(finish(True)))
        after()

    @pl.when(i == last)
    def _last():
        finish(False)
        after()


def _ffn_matmuls(x_ref, wg_v, wu_v, wd_v, ff_keep, zero):
    xb = x_ref[...].astype(BF16)
    gate = _dot(xb, wg_v[...])
    up = _dot(xb, wu_v[...])
    one = 1.0 + zero
    h = (gate * (one / (one + jnp.exp(-gate))) * up).astype(BF16)
    ff_keep[...] = _dot(h, wd_v[...])


def _ffn_kernel(alpha, li, hi, ple, weights, casts, *refs):
    main, w, load, o_ref, (ff_keep,) = _split_refs(refs, 5 if ple else 4, weights, casts)
    x_ref, xprev_ref, lng_ref, lnb_ref = main[:4]
    s = 2 * hi

    def store(rows, y):
        o_ref[rows, :] = y

    def finish(chained):
        return _residual_ln(alpha, 0.5, xprev_ref, ff_keep,
                            lng_ref[li, s:s + 1, :], lnb_ref[li, s:s + 1, :], store, chained)

    def embed():
        part = o_ref.shape[0] // EMBED_PARTS
        for k in range(EMBED_PARTS):
            rows = pl.ds(k * part, part)
            y = o_ref[rows, :]
            emb = _dot(main[4][rows, :].astype(BF16), w[3][...])
            gate = jax.nn.sigmoid(_dot(y.astype(BF16), w[4][...]))
            o_ref[rows, :] = y + emb * gate

    matmuls = functools.partial(_ffn_matmuls, x_ref, w[0], w[1], w[2], ff_keep)
    if ple:
        _staggered(load, matmuls, finish, embed)
    else:
        _staggered(load, matmuls, finish)


def _ffn_call(x, ln_g, ln_b, alpha, li, hi, tm, weights, casts, p=None):
    n, d = x.shape
    tiles = n // tm
    cur, prev = _staggered_maps(tiles)
    main_specs = [pl.BlockSpec((tm, d), cur), pl.BlockSpec((tm, d), prev),
                  _resident(ln_g.shape), _resident(ln_b.shape)]
    main_args = [x, x, ln_g, ln_b]
    if p is not None:
        main_specs.append(pl.BlockSpec((None, tm, p.shape[-1]), lambda i: (li,) + prev(i)))
        main_args.append(p)
    body = functools.partial(_ffn_kernel, alpha, li, hi, p is not None, weights, casts)
    scratch = [pltpu.VMEM((tm, d), F32)]
    return _call(body, "ffn_ln" if p is None else "ffn_ln_ple", tiles, tm, d,
                 main_specs, main_args, weights, casts, scratch)


def _mixer_ab_kernel(alpha, li, ji, tm, tiles_per_seq, weights, casts, *refs):
    main, (win_v, wout_v), load_weights, o_ref, scratch = _split_refs(refs, 13, weights, casts)
    (sinks_ref, x_ref, xprev_ref, pos_ref, inv_ref, sgn_ref, sg_ref, sb_ref, ws_ref, bs_ref,
     band_ref, lng_ref, lnb_ref) = main
    q_scr, kv_scr, u_scr, sv_scr, wcat_scr, cat_scr = scratch
    i = pl.program_id(0)
    last = pl.num_programs(0) - 1
    lane = lax.broadcasted_iota(jnp.int32, (1, LANES), 1)
    low_half = lane < HEAD_DIM
    slot = i % 2

    def load():
        load_weights()
        r = lax.broadcasted_iota(jnp.int32, (CHUNK, CHUNK), 0)
        c = lax.broadcasted_iota(jnp.int32, (CHUNK, CHUNK), 1)
        tril = r >= c
        for j in range(SGU_GROUPS // 2):
            a = jnp.where(tril, ws_ref[ji, 2 * j], 0.0)
            bb = jnp.where(tril, ws_ref[ji, 2 * j + 1], 0.0)
            wcat_scr[j] = jnp.concatenate([a, bb], axis=1).astype(BF16)

    def project():
        _mixer_ab_project(ji, tm, (i % tiles_per_seq) == 0, low_half, lane, x_ref, pos_ref,
                          win_v, inv_ref, sgn_ref, sg_ref, sb_ref,
                          q_scr.at[slot], kv_scr.at[slot], kv_scr.at[1 - slot],
                          u_scr.at[slot], sv_scr.at[slot])

    def attend():
        _mixer_ab_attend(ji, tm, ((i - 1) % tiles_per_seq) == 0, low_half, sinks_ref,
                         bs_ref, band_ref, q_scr.at[1 - slot], kv_scr.at[1 - slot],
                         u_scr.at[1 - slot], sv_scr.at[1 - slot], wcat_scr, cat_scr)
        part = tm // OUT_PARTS
        for k in range(OUT_PARTS):
            rows = pl.ds(k * part, part)
            mix = _dot(cat_scr[rows, :], wout_v[...])
            o_ref[rows, :] = _layer_norm(alpha * xprev_ref[rows, :] + mix,
                                         lng_ref[li, 1:2, :], lnb_ref[li, 1:2, :])

    @pl.when(i == 0)
    def _first():
        load()
        project()

    @pl.when(jnp.logical_and(i > 0, i < last))
    def _steady():
        attend()
        project()

    @pl.when(i == last)
    def _last():
        attend()


def _mixer_ab_project(ji, tm, first_tile, low_half, lane, x_ref, pos_ref, win_v,
                      inv_ref, sgn_ref, sg_ref, sb_ref, q_scr, kv_scr, kv_other, u_scr, sv_scr):
    x = x_ref[...]
    h = _dot(x.astype(BF16), win_v[...])

    nseg = LANES // ROT_DIM
    seg = tm // nseg
    pos = pos_ref[...].astype(F32)
    packed = jnp.zeros((seg, LANES), F32)
    for s in range(nseg):
        in_seg = (lane // ROT_DIM) == s
        packed = jnp.where(in_seg, pos[:, s:s + 1], packed)
    ang = packed * inv_ref[...]
    cos_p = jnp.cos(ang)
    sin_p = jnp.sin(ang) * sgn_ref[...]
    rot_lo = lane < ROT_DIM
    rot_hi = jnp.logical_and(lane >= HEAD_DIM, lane < HEAD_DIM + ROT_DIM)

    def unpack(t, fill):
        parts = []
        for s in range(nseg):
            lo = t if s == 0 else pltpu.roll(t, LANES - s * ROT_DIM, axis=1)
            shift = (HEAD_DIM - s * ROT_DIM) % LANES
            hi = t if shift == 0 else pltpu.roll(t, shift, axis=1)
            parts.append(jnp.where(rot_lo, lo, jnp.where(rot_hi, hi, fill)))
        return jnp.concatenate(parts, axis=0)

    cos_t = unpack(cos_p, 1.0)
    sin_t = unpack(sin_p, 0.0)
    take_up = (lane % HEAD_DIM) < (ROT_DIM // 2)

    def rotary(t):
        up = pltpu.roll(t, LANES - ROT_DIM // 2, axis=1)
        dn = pltpu.roll(t, ROT_DIM // 2, axis=1)
        return t * cos_t + jnp.where(take_up, up, dn) * sin_t

    scale = HEAD_DIM ** -0.5
    for j in range(Q_W // LANES):
        qj = rotary(h[:, j * LANES:(j + 1) * LANES]) * scale
        q_scr[:, j * LANES:(j + 1) * LANES] = qj.astype(BF16)

    def dup_heads(t):
        sw = pltpu.roll(t, HEAD_DIM, axis=1)
        return jnp.where(low_half, t, sw), jnp.where(low_half, sw, t)

    k0, k1 = dup_heads(rotary(h[:, Q_W:Q_W + KV_W]))
    v0, v1 = dup_heads(h[:, Q_W + KV_W:Q_W + 2 * KV_W])
    kv_scr[WINDOW:, 0 * LANES:1 * LANES] = k0.astype(BF16)
    kv_scr[WINDOW:, 1 * LANES:2 * LANES] = k1.astype(BF16)
    kv_scr[WINDOW:, 2 * LANES:3 * LANES] = v0.astype(BF16)
    kv_scr[WINDOW:, 3 * LANES:4 * LANES] = v1.astype(BF16)

    su0 = Q_W + 2 * KV_W
    u_scr[...] = _gelu_tanh(h[:, su0:su0 + SGU_W])
    sv = _layer_norm(_gelu_tanh(h[:, su0 + SGU_W:su0 + 2 * SGU_W]),
                     sg_ref[ji:ji + 1, :], sb_ref[ji:ji + 1, :])
    sv_scr[...] = sv.astype(BF16)

    halo = kv_other[tm:tm + WINDOW, :]
    kv_scr[0:WINDOW, :] = jnp.where(first_tile, jnp.zeros_like(halo), halo)


def _mixer_ab_attend(ji, tm, first_tile, low_half, sinks_ref, bs_ref, band_ref,
                     q_scr, kv_scr, u_scr, sv_scr, wcat_scr, cat_scr):
    nblk = tm // WINDOW
    row_first = lax.broadcasted_iota(jnp.int32, (2 * WINDOW, 1), 0) < WINDOW
    before_start = lax.broadcasted_iota(jnp.int32, (1, 2 * WINDOW), 1) < WINDOW
    first_bias = jnp.where(jnp.logical_and(first_tile, before_start), NEG_INF, band_ref[...])

    def block(bi):
        r0 = bi * WINDOW
        for kvh in range(ATT_KV_HEADS):
            kband = kv_scr[pl.ds(r0, 2 * WINDOW), kvh * LANES:(kvh + 1) * LANES]
            vband = kv_scr[pl.ds(r0, 2 * WINDOW), (2 + kvh) * LANES:(3 + kvh) * LANES]
            for pr in range(2):
                slab = kvh * 2 + pr
                qp = q_scr[pl.ds(r0, WINDOW), slab * LANES:(slab + 1) * LANES]
                blank = jnp.zeros_like(qp)
                qs = jnp.concatenate([jnp.where(low_half, qp, blank),
                                      jnp.where(low_half, blank, qp)], axis=0)
                s = lax.dot_general(qs, kband, (((1,), (1,)), ((), ())),
                                    preferred_element_type=F32)
                s = s + (first_bias if bi == 0 else band_ref[...])
                sink = jnp.where(row_first, sinks_ref[ji, 2 * slab], sinks_ref[ji, 2 * slab + 1])
                m = jnp.max(s, axis=-1, keepdims=True)
                p = jnp.exp(s - m)
                ov = _dot(p.astype(BF16), jnp.concatenate([vband, jnp.ones_like(vband)], axis=1))
                o = ov[:, :LANES] / (ov[:, LANES:] + jnp.exp(sink - m))
                att = jnp.where(low_half, o[:WINDOW], o[WINDOW:])
                cat_scr[pl.ds(r0, WINDOW), slab * LANES:(slab + 1) * LANES] = att.astype(BF16)
        for j in range(SGU_GROUPS // 2):
            vp = sv_scr[pl.ds(r0, CHUNK), j * LANES:(j + 1) * LANES]
            blank = jnp.zeros_like(vp)
            rhs = jnp.concatenate([jnp.where(low_half, vp, blank),
                                   jnp.where(low_half, blank, vp)], axis=0)
            mixed = _dot(wcat_scr[j], rhs) + bs_ref[:, j * LANES:(j + 1) * LANES]
            out = u_scr[pl.ds(r0, CHUNK), j * LANES:(j + 1) * LANES] * mixed
            cat_scr[pl.ds(r0, CHUNK), Q_W + j * LANES:Q_W + (j + 1) * LANES] = out.astype(BF16)

    for bi in range(nblk):
        block(bi)


def _mixer_ab_call(x, pos, sinks, inv_lane, sgn_lane, sgu_g, sgu_b, w_s, bias_t, ln_g, ln_b,
                   alpha, li, ji, tm, seq, weights, casts):
    n, d = x.shape
    tiles = n // tm
    body = functools.partial(_mixer_ab_kernel, alpha, li, ji, tm, seq // tm, weights, casts)
    cur, prev = _staggered_maps(tiles)
    main_specs = [
        pl.BlockSpec(memory_space=pltpu.SMEM),
        pl.BlockSpec((tm, d), cur),
        pl.BlockSpec((tm, d), prev),
        pl.BlockSpec((tm // pos.shape[1], pos.shape[1]), cur),
        _resident((1, LANES)), _resident((1, LANES)),
        _resident(sgu_g.shape), _resident(sgu_b.shape),
        _resident(w_s.shape),
        _resident((CHUNK, SGU_W)),
        _resident((2 * WINDOW, 2 * WINDOW)),
        _resident(ln_g.shape), _resident(ln_b.shape),
    ]
    qi = jnp.arange(2 * WINDOW)[:, None] % WINDOW
    kj = jnp.arange(2 * WINDOW)[None, :]
    band = jnp.where((kj <= qi + WINDOW) & (kj > qi), 0.0, NEG_INF).astype(F32)
    main_args = [sinks, x, x, pos, inv_lane, sgn_lane, sgu_g, sgu_b, w_s, bias_t, band, ln_g, ln_b]
    scratch = [
        pltpu.VMEM((2, tm, Q_W), BF16),
        pltpu.VMEM((2, tm + WINDOW, 4 * LANES), BF16),
        pltpu.VMEM((2, tm, SGU_W), F32),
        pltpu.VMEM((2, tm, SGU_W), BF16),
        pltpu.VMEM((SGU_GROUPS // 2, CHUNK, 2 * CHUNK), BF16),
        pltpu.VMEM((tm, Q_W + SGU_W), BF16),
    ]
    return _call(body, "mixer_attn_sgu", tiles, tm, d, main_specs, main_args, weights, casts,
                 scratch)


def _mixer_conv_kernel(alpha, li, ji, tm, tiles_per_seq, weights, casts, *refs):
    main, (win_v, wout_v), load, o_ref, scratch = _split_refs(refs, 5, weights, casts)
    x_ref, xprev_ref, cw_ref, lng_ref, lnb_ref = main
    cz_scr, mix_keep = scratch
    i = pl.program_id(0)
    d = x_ref.shape[1]
    pad = SUBLANES

    @pl.when((i % tiles_per_seq) == 0)
    def _reset_halo():
        cz_scr[0:pad, :] = jnp.zeros((pad, d), F32)

    def matmuls(zero):
        h = _dot(x_ref[...].astype(BF16), win_v[...])
        cz = h[:, d:2 * d] * h[:, 2 * d:3 * d]
        left = cz_scr[...]
        row = lax.broadcasted_iota(jnp.int32, (tm, 1), 0)
        y = (cw_ref[ji, CONV_WIDTH - 1:CONV_WIDTH, :] + zero) * cz
        for t in range(CONV_WIDTH - 1):
            back = CONV_WIDTH - 1 - t
            shifted = pltpu.roll(cz, back, axis=0)
            for r in range(back):
                shifted = jnp.where(row == r, left[pad - back + r:pad - back + r + 1, :], shifted)
            y = y + cw_ref[ji, t:t + 1, :] * shifted
        cz_scr[...] = cz[tm - pad:, :]
        mix_keep[...] = _dot((h[:, 0:d] * y).astype(BF16), wout_v[...])

    def finish(chained):
        def store(rows, y):
            o_ref[rows, :] = y
        return _residual_ln(alpha, 1.0, xprev_ref, mix_keep,
                            lng_ref[li, 1:2, :], lnb_ref[li, 1:2, :], store, chained)

    _staggered(load, matmuls, finish)


def _mixer_conv_call(x, conv_w, ln_g, ln_b, alpha, li, ji, tm, seq, weights, casts):
    n, d = x.shape
    tiles = n // tm
    body = functools.partial(_mixer_conv_kernel, alpha, li, ji, tm, seq // tm, weights, casts)
    cur, prev = _staggered_maps(tiles)
    main_specs = [pl.BlockSpec((tm, d), cur), pl.BlockSpec((tm, d), prev),
                  _resident(conv_w.shape), _resident(ln_g.shape), _resident(ln_b.shape)]
    scratch = [pltpu.VMEM((SUBLANES, d), F32),
               pltpu.VMEM((tm, d), F32)]
    return _call(body, "mixer_conv", tiles, tm, d, main_specs, [x, x, conv_w, ln_g, ln_b],
                 weights, casts, scratch)


def kernel(x, p, positions, ln_g, ln_b, ffn_w_gate, ffn_w_up, ffn_w_down, ab_w_in, ab_sinks,
           sgu_ln_g, sgu_ln_b, sgu_w_s, sgu_b_s, ab_w_out, sc_w_in, sc_conv_w, sc_w_out,
           ple_w_proj, ple_w_gate):
    bsz, seq, d = x.shape
    depth = p.shape[0]
    alpha = (2 * depth) ** 0.25
    tm = TM
    assert seq % tm == 0 and tm % WINDOW == 0
    assert sc_conv_w.shape[1] == CONV_WIDTH

    n = bsz * seq
    steps = n // tm + 1
    xs = x.reshape(n, d)
    nseg = LANES // ROT_DIM
    pos = positions.reshape(n // tm, nseg, tm // nseg).transpose(0, 2, 1).reshape(n // nseg, nseg)
    ps = p.reshape(depth, n, p.shape[-1])

    half = ROT_DIM // 2
    r = jnp.arange(LANES) % ROT_DIM
    inv_lane = jnp.power(ROPE_THETA, -(r % half).astype(F32) * (2.0 / ROT_DIM)).reshape(1, LANES)
    sgn_lane = jnp.where(r < half, -1.0, 1.0).reshape(1, LANES).astype(F32)

    calls = []
    for i in range(depth):
        j = i // 2
        ffn = lambda h, i=i: [(ffn_w_gate, (i, h)), (ffn_w_up, (i, h)), (ffn_w_down, (i, h))]
        calls.append(("ffn", i, 0, ffn(0)))
        if i % 2 == 0:
            calls.append(("attn", i, j, [(ab_w_in, (j,)), (ab_w_out, (j,))]))
        else:
            calls.append(("conv", i, j, [(sc_w_in, (j,)), (sc_w_out, (j,))]))
        calls.append(("ple", i, 1, ffn(1) + [(ple_w_proj, (i,)), (ple_w_gate, (i,))]))

    given = None
    for k, (kind, i, j, mats) in enumerate(calls):
        weights = _Weights(mats, given)
        casts = _Casts(calls[k + 1][3] if k + 1 < len(calls) else [], steps)
        if kind == "ffn":
            xs, given = _ffn_call(xs, ln_g, ln_b, alpha, i, j, tm, weights, casts)
        elif kind == "ple":
            xs, given = _ffn_call(xs, ln_g, ln_b, alpha, i, j, tm, weights, casts, p=ps)
        elif kind == "attn":
            bias_t = jnp.repeat(jnp.transpose(sgu_b_s[j]), SGU_GROUP_DIM, axis=1)
            xs, given = _mixer_ab_call(xs, pos, ab_sinks, inv_lane, sgn_lane, sgu_ln_g, sgu_ln_b,
                                       sgu_w_s, bias_t, ln_g, ln_b, alpha, i, j, tm, seq,
                                       weights, casts)
        else:
            xs, given = _mixer_conv_call(xs, sc_conv_w, ln_g, ln_b, alpha, i, j, tm, seq,
                                         weights, casts)
    return xs.reshape(bsz, seq, d)
```

```python
import functools
import math

import jax
import jax.numpy as jnp
from jax import lax
from jax.experimental import pallas as pl
from jax.experimental.pallas import tpu as pltpu

ATT_HEADS = 8
ATT_KV_HEADS = 2
HEAD_DIM = 64
WINDOW = 128
ROT_DIM = HEAD_DIM // 4
ROPE_THETA = 500000.0
SGU_GROUPS = 8
SGU_GROUP_DIM = 64
CHUNK = 128
CONV_WIDTH = 3
LN_EPS = 1e-5
NEG_INF = -1e30

Q_W = ATT_HEADS * HEAD_DIM
KV_W = ATT_KV_HEADS * HEAD_DIM
SGU_W = SGU_GROUPS * SGU_GROUP_DIM

LANES = 128
SUBLANES = 8
BF16_ROWS = 16
VMEM_LIMIT = 56 * 1024 * 1024
STAGE_BYTES = 3 * 512 * 1024
STAGE_SLOTS = 6
TM = 512
EMBED_PARTS = 2
OUT_PARTS = 2
RING_AHEAD = 2
RING_SLOTS = RING_AHEAD + 2
LN_ROWS = 8
LN_AHEAD = 6

BF16 = jnp.bfloat16
F32 = jnp.float32


def _dot(a, b):
    return jnp.dot(a, b, preferred_element_type=F32)


def _layer_norm(y, g, b):
    mu = jnp.mean(y, axis=-1, keepdims=True)
    d = y - mu
    var = jnp.mean(d * d, axis=-1, keepdims=True)
    return d * lax.rsqrt(var + LN_EPS) * g + b


def _gelu_tanh(x):
    c = math.sqrt(2.0 / math.pi)
    half_x = 0.5 * x
    return half_x * jnp.tanh(x * ((c * 0.044715) * (x * x) + c)) + half_x


def _resident(shape):
    nd = len(shape)
    return pl.BlockSpec(shape, lambda i: (0,) * nd, pipeline_mode=pl.Buffered(1))


_HBM = pl.BlockSpec(memory_space=pl.ANY)


def _params():
    return pltpu.CompilerParams(dimension_semantics=("arbitrary",),
                                vmem_limit_bytes=VMEM_LIMIT)


def _stage(cols):
    rows = 1 << ((STAGE_BYTES // (4 * cols)).bit_length() - 1)
    return [pltpu.VMEM((STAGE_SLOTS, rows, cols), F32), pltpu.SemaphoreType.DMA((STAGE_SLOTS,))]


def _fetch_cast(jobs):
    chunks = []
    used = {}
    for src, dst, stage, sem in jobs:
        total, cols = src.shape
        rows = min(stage.shape[1], total)
        assert total % rows == 0 and cols == stage.shape[2] and dst.shape == src.shape
        for c in range(total // rows):
            slot = used.get(id(stage), 0) % STAGE_SLOTS
            used[id(stage)] = used.get(id(stage), 0) + 1
            staged = stage.at[slot, pl.ds(0, rows), :]
            copy = pltpu.make_async_copy(src.at[pl.ds(c * rows, rows), :], staged, sem.at[slot])
            chunks.append((copy, staged, dst.at[pl.ds(c * rows, rows), :]))
    for copy, _, _ in chunks[:STAGE_SLOTS]:
        copy.start()
    for n, (copy, staged, out) in enumerate(chunks):
        copy.wait()
        out[...] = staged[...].astype(BF16)
        if n + STAGE_SLOTS < len(chunks):
            chunks[n + STAGE_SLOTS][0].start()


class _Weights:
    def __init__(self, mats, given):
        self.mats, self.given = mats, given
        self.shapes = [tuple(a.shape[-2:]) for a, _ in mats]
        self.widths = sorted({s[1] for s in self.shapes})

    @property
    def args(self):
        return list(self.given) if self.given else [a for a, _ in self.mats]

    @property
    def in_specs(self):
        return [_resident(s) for s in self.shapes] if self.given else [_HBM] * len(self.mats)

    @property
    def scratch(self):
        if self.given:
            return []
        out = [pltpu.VMEM(s, BF16) for s in self.shapes]
        for w in self.widths:
            out += _stage(w)
        return out

    def bind(self, in_refs, scratch_refs):
        if self.given:
            return list(in_refs), (lambda: None)
        n = len(self.mats)
        stages = {w: (scratch_refs[n + 2 * k], scratch_refs[n + 2 * k + 1])
                  for k, w in enumerate(self.widths)}
        jobs = [(src.at[lead], dst) + stages[shape[1]]
                for src, (_, lead), dst, shape in zip(in_refs, self.mats, scratch_refs, self.shapes)]
        return list(scratch_refs[:n]), (lambda: _fetch_cast(jobs))


class _Casts:
    def __init__(self, mats, steps):
        self.mats = mats
        self.blocks = []
        for a, _ in mats:
            rows, cols = a.shape[-2:]
            rb = next(r for r in range(BF16_ROWS, rows + 1, BF16_ROWS)
                      if rows % r == 0 and rows // r <= steps)
            self.blocks.append((rb, rows // rb, rows, cols))

    @property
    def args(self):
        return [a for a, _ in self.mats]

    @property
    def in_specs(self):
        return [pl.BlockSpec((None,) * len(lead) + (rb, cols),
                             lambda i, lead=lead, nb=nb: lead + (jnp.minimum(i, nb - 1), 0))
                for (_, lead), (rb, nb, _, cols) in zip(self.mats, self.blocks)]

    @property
    def out_specs(self):
        return [pl.BlockSpec((rb, cols), lambda i, nb=nb: (jnp.minimum(i, nb - 1), 0))
                for rb, nb, _, cols in self.blocks]

    @property
    def out_shapes(self):
        return [jax.ShapeDtypeStruct((rows, cols), BF16) for _, _, rows, cols in self.blocks]

    @staticmethod
    def run(in_refs, out_refs):
        for src, dst in zip(in_refs, out_refs):
            dst[...] = src[...].astype(BF16)


def _split_refs(refs, n_main, weights, casts):
    nw, nc = len(weights.mats), len(casts.mats)
    main, rest = refs[:n_main], refs[n_main:]
    w_in, c_in = rest[:nw], rest[nw:nw + nc]
    o_ref, c_out = rest[nw + nc], rest[nw + nc + 1:nw + 2 * nc + 1]
    scratch = rest[nw + 2 * nc + 1:]
    ns = len(weights.scratch)
    w, load = weights.bind(w_in, scratch[:ns])
    _Casts.run(c_in, c_out)
    return main, w, load, o_ref, scratch[ns:]


def _call(body, name, tiles, tm, d, main_specs, main_args, weights, casts, scratch):
    outs = pl.pallas_call(
        body,
        grid=(tiles + 1,),
        in_specs=main_specs + weights.in_specs + casts.in_specs,
        out_specs=[pl.BlockSpec((tm, d), _staggered_maps(tiles)[1])] + casts.out_specs,
        out_shape=[jax.ShapeDtypeStruct((tiles * tm, d), F32)] + casts.out_shapes,
        scratch_shapes=weights.scratch + scratch,
        compiler_params=_params(),
        name=name,
    )(*main_args, *weights.args, *casts.args)
    return outs[0], list(outs[1:])


def _fold_bits(y):
    bits = lax.bitcast_convert_type(y, jnp.int32)
    cols = bits[:, 0:LANES]
    for j in range(1, y.shape[1] // LANES):
        cols = cols | bits[:, j * LANES:(j + 1) * LANES]
    rows = cols[0:SUBLANES]
    for k in range(1, y.shape[0] // SUBLANES):
        rows = rows | cols[k * SUBLANES:(k + 1) * SUBLANES]
    return rows


def _zero_from(bits):
    top = jnp.max(bits, axis=(0, 1), keepdims=True)
    cleared = lax.shift_right_logical(lax.shift_right_logical(top, 16), 16)
    return lax.bitcast_convert_type(cleared, F32)


def _residual_ln(alpha, beta, x_keep, r_keep, g, b, store, chained):
    if not chained:
        beta_r = r_keep[...] if beta == 1.0 else beta * r_keep[...]
        store(slice(None), _layer_norm(alpha * x_keep[...] + beta_r, g, b))
        return None
    ngroups = x_keep.shape[0] // LN_ROWS
    width = x_keep.shape[1]
    folded = []
    for c in range(ngroups):
        rows = pl.ds(c * LN_ROWS, LN_ROWS)
        a = alpha
        if c >= LN_AHEAD:
            z = lax.shift_right_logical(lax.shift_right_logical(folded[c - LN_AHEAD], 16), 16)
            a = alpha + jnp.tile(lax.bitcast_convert_type(z, F32),
                                 (LN_ROWS // SUBLANES, width // LANES))
        r = r_keep[rows, :]
        y = _layer_norm(a * x_keep[rows, :] + (r if beta == 1.0 else beta * r), g, b)
        store(rows, y)
        folded.append(_fold_bits(y))
    bits = folded[0]
    for f in folded[1:]:
        bits = bits | f
    return bits


def _staggered_maps(tiles):
    return (lambda i: (jnp.minimum(i, tiles - 1), 0)), (lambda i: (jnp.maximum(i - 1, 0), 0))


def _staggered(load, matmuls, finish, after=lambda: None):
    i = pl.program_id(0)
    last = pl.num_programs(0) - 1

    @pl.when(i == 0)
    def _first():
        load()
        matmuls(0.0)

    @pl.when(jnp.logical_and(i > 0, i < last))
    def _steady():
        matmuls(_zero_from(finish(True)))
        after()

    @pl.when(i == last)
    def _last():
        finish(False)
        after()


def _ffn_matmuls(x_ref, wg_v, wu_v, wd_v, ff_keep, zero):
    xb = x_ref[...].astype(BF16)
    gate = _dot(xb, wg_v[...])
    up = _dot(xb, wu_v[...])
    one = 1.0 + zero
    h = (gate * (one / (one + jnp.exp(-gate))) * up).astype(BF16)
    ff_keep[...] = _dot(h, wd_v[...])


def _ffn_kernel(alpha, li, hi, ple, weights, casts, *refs):
    main, w, load, o_ref, (ff_keep,) = _split_refs(refs, 5 if ple else 4, weights, casts)
    x_ref, xprev_ref, lng_ref, lnb_ref = main[:4]
    s = 2 * hi

    def store(rows, y):
        o_ref[rows, :] = y

    def finish(chained):
        return _residual_ln(alpha, 0.5, xprev_ref, ff_keep,
                            lng_ref[li, s:s + 1, :], lnb_ref[li, s:s + 1, :], store, chained)

    def embed():
        part = o_ref.shape[0] // EMBED_PARTS
        for k in range(EMBED_PARTS):
            rows = pl.ds(k * part, part)
            y = o_ref[rows, :]
            emb = _dot(main[4][rows, :].astype(BF16), w[3][...])
            gate = jax.nn.sigmoid(_dot(y.astype(BF16), w[4][...]))
            o_ref[rows, :] = y + emb * gate

    matmuls = functools.partial(_ffn_matmuls, x_ref, w[0], w[1], w[2], ff_keep)
    if ple:
        _staggered(load, matmuls, finish, embed)
    else:
        _staggered(load, matmuls, finish)


def _ffn_call(x, ln_g, ln_b, alpha, li, hi, tm, weights, casts, p=None):
    n, d = x.shape
    tiles = n // tm
    cur, prev = _staggered_maps(tiles)
    main_specs = [pl.BlockSpec((tm, d), cur), pl.BlockSpec((tm, d), prev),
                  _resident(ln_g.shape), _resident(ln_b.shape)]
    main_args = [x, x, ln_g, ln_b]
    if p is not None:
        main_specs.append(pl.BlockSpec((None, tm, p.shape[-1]), lambda i: (li,) + prev(i)))
        main_args.append(p)
    body = functools.partial(_ffn_kernel, alpha, li, hi, p is not None, weights, casts)
    scratch = [pltpu.VMEM((tm, d), F32)]
    return _call(body, "ffn_ln" if p is None else "ffn_ln_ple", tiles, tm, d,
                 main_specs, main_args, weights, casts, scratch)


def _mixer_ab_kernel(alpha, li, ji, tm, tiles_per_seq, weights, casts, *refs):
    main, (win_v, wout_v), load_weights, o_ref, scratch = _split_refs(refs, 13, weights, casts)
    (sinks_ref, x_ref, xprev_ref, pos_ref, inv_ref, sgn_ref, sg_ref, sb_ref, ws_ref, bs_ref,
     band_ref, lng_ref, lnb_ref) = main
    q_scr, kv_scr, u_scr, sv_scr, wcat_scr, cat_scr = scratch
    i = pl.program_id(0)
    last = pl.num_programs(0) - 1
    lane = lax.broadcasted_iota(jnp.int32, (1, LANES), 1)
    low_half = lane < HEAD_DIM
    slot = i % 2

    def load():
        load_weights()
        r = lax.broadcasted_iota(jnp.int32, (CHUNK, CHUNK), 0)
        c = lax.broadcasted_iota(jnp.int32, (CHUNK, CHUNK), 1)
        tril = r >= c
        for j in range(SGU_GROUPS // 2):
            a = jnp.where(tril, ws_ref[ji, 2 * j], 0.0)
            bb = jnp.where(tril, ws_ref[ji, 2 * j + 1], 0.0)
            wcat_scr[j] = jnp.concatenate([a, bb], axis=1).astype(BF16)

    def project():
        _mixer_ab_project(ji, tm, (i % tiles_per_seq) == 0, low_half, lane, x_ref, pos_ref,
                          win_v, inv_ref, sgn_ref, sg_ref, sb_ref,
                          q_scr.at[slot], kv_scr.at[slot], kv_scr.at[1 - slot],
                          u_scr.at[slot], sv_scr.at[slot])

    def attend():
        _mixer_ab_attend(ji, tm, ((i - 1) % tiles_per_seq) == 0, low_half, sinks_ref,
                         bs_ref, band_ref, q_scr.at[1 - slot], kv_scr.at[1 - slot],
                         u_scr.at[1 - slot], sv_scr.at[1 - slot], wcat_scr, cat_scr)
        part = tm // OUT_PARTS
        for k in range(OUT_PARTS):
            rows = pl.ds(k * part, part)
            mix = _dot(cat_scr[rows, :], wout_v[...])
            o_ref[rows, :] = _layer_norm(alpha * xprev_ref[rows, :] + mix,
                                         lng_ref[li, 1:2, :], lnb_ref[li, 1:2, :])

    @pl.when(i == 0)
    def _first():
        load()
        project()

    @pl.when(jnp.logical_and(i > 0, i < last))
    def _steady():
        attend()
        project()

    @pl.when(i == last)
    def _last():
        attend()


def _mixer_ab_project(ji, tm, first_tile, low_half, lane, x_ref, pos_ref, win_v,
                      inv_ref, sgn_ref, sg_ref, sb_ref, q_scr, kv_scr, kv_other, u_scr, sv_scr):
    x = x_ref[...]
    h = _dot(x.astype(BF16), win_v[...])

    nseg = LANES // ROT_DIM
    seg = tm // nseg
    pos = pos_ref[...].astype(F32)
    packed = jnp.zeros((seg, LANES), F32)
    for s in range(nseg):
        in_seg = (lane // ROT_DIM) == s
        packed = jnp.where(in_seg, pos[:, s:s + 1], packed)
    ang = packed * inv_ref[...]
    cos_p = jnp.cos(ang)
    sin_p = jnp.sin(ang) * sgn_ref[...]
    rot_lo = lane < ROT_DIM
    rot_hi = jnp.logical_and(lane >= HEAD_DIM, lane < HEAD_DIM + ROT_DIM)

    def unpack(t, fill):
        parts = []
        for s in range(nseg):
            lo = t if s == 0 else pltpu.roll(t, LANES - s * ROT_DIM, axis=1)
            shift = (HEAD_DIM - s * ROT_DIM) % LANES
            hi = t if shift == 0 else pltpu.roll(t, shift, axis=1)
            parts.append(jnp.where(rot_lo, lo, jnp.where(rot_hi, hi, fill)))
        return jnp.concatenate(parts, axis=0)

    cos_t = unpack(cos_p, 1.0)
    sin_t = unpack(sin_p, 0.0)
    take_up = (lane % HEAD_DIM) < (ROT_DIM // 2)

    def rotary(t):
        up = pltpu.roll(t, LANES - ROT_DIM // 2, axis=1)
        dn = pltpu.roll(t, ROT_DIM // 2, axis=1)
        return t * cos_t + jnp.where(take_up, up, dn) * sin_t

    scale = HEAD_DIM ** -0.5
    for j in range(Q_W // LANES):
        qj = rotary(h[:, j * LANES:(j + 1) * LANES]) * scale
        q_scr[:, j * LANES:(j + 1) * LANES] = qj.astype(BF16)

    def dup_heads(t):
        sw = pltpu.roll(t, HEAD_DIM, axis=1)
        return jnp.where(low_half, t, sw), jnp.where(low_half, sw, t)

    k0, k1 = dup_heads(rotary(h[:, Q_W:Q_W + KV_W]))
    v0, v1 = dup_heads(h[:, Q_W + KV_W:Q_W + 2 * KV_W])
    kv_scr[WINDOW:, 0 * LANES:1 * LANES] = k0.astype(BF16)
    kv_scr[WINDOW:, 1 * LANES:2 * LANES] = k1.astype(BF16)
    kv_scr[WINDOW:, 2 * LANES:3 * LANES] = v0.astype(BF16)
    kv_scr[WINDOW:, 3 * LANES:4 * LANES] = v1.astype(BF16)

    su0 = Q_W + 2 * KV_W
    u_scr[...] = _gelu_tanh(h[:, su0:su0 + SGU_W])
    sv = _layer_norm(_gelu_tanh(h[:, su0 + SGU_W:su0 + 2 * SGU_W]),
                     sg_ref[ji:ji + 1, :], sb_ref[ji:ji + 1, :])
    sv_scr[...] = sv.astype(BF16)

    halo = kv_other[tm:tm + WINDOW, :]
    kv_scr[0:WINDOW, :] = jnp.where(first_tile, jnp.zeros_like(halo), halo)


def _mixer_ab_attend(ji, tm, first_tile, low_half, sinks_ref, bs_ref, band_ref,
                     q_scr, kv_scr, u_scr, sv_scr, wcat_scr, cat_scr):
    nblk = tm // WINDOW
    row_first = lax.broadcasted_iota(jnp.int32, (2 * WINDOW, 1), 0) < WINDOW
    before_start = lax.broadcasted_iota(jnp.int32, (1, 2 * WINDOW), 1) < WINDOW
    first_bias = jnp.where(jnp.logical_and(first_tile, before_start), NEG_INF, band_ref[...])

    def block(bi):
        r0 = bi * WINDOW
        for kvh in range(ATT_KV_HEADS):
            kband = kv_scr[pl.ds(r0, 2 * WINDOW), kvh * LANES:(kvh + 1) * LANES]
            vband = kv_scr[pl.ds(r0, 2 * WINDOW), (2 + kvh) * LANES:(3 + kvh) * LANES]
            for pr in range(2):
                slab = kvh * 2 + pr
                qp = q_scr[pl.ds(r0, WINDOW), slab * LANES:(slab + 1) * LANES]
                blank = jnp.zeros_like(qp)
                qs = jnp.concatenate([jnp.where(low_half, qp, blank),
                                      jnp.where(low_half, blank, qp)], axis=0)
                s = lax.dot_general(qs, kband, (((1,), (1,)), ((), ())),
                                    preferred_element_type=F32)
                s = s + (first_bias if bi == 0 else band_ref[...])
                sink = jnp.where(row_first, sinks_ref[ji, 2 * slab], sinks_ref[ji, 2 * slab + 1])
                m = jnp.max(s, axis=-1, keepdims=True)
                p = jnp.exp(s - m)
                ov = _dot(p.astype(BF16), jnp.concatenate([vband, jnp.ones_like(vband)], axis=1))
                o = ov[:, :LANES] / (ov[:, LANES:] + jnp.exp(sink - m))
                att = jnp.where(low_half, o[:WINDOW], o[WINDOW:])
                cat_scr[pl.ds(r0, WINDOW), slab * LANES:(slab + 1) * LANES] = att.astype(BF16)
        for j in range(SGU_GROUPS // 2):
            vp = sv_scr[pl.ds(r0, CHUNK), j * LANES:(j + 1) * LANES]
            blank = jnp.zeros_like(vp)
            rhs = jnp.concatenate([jnp.where(low_half, vp, blank),
                                   jnp.where(low_half, blank, vp)], axis=0)
            mixed = _dot(wcat_scr[j], rhs) + bs_ref[:, j * LANES:(j + 1) * LANES]
            out = u_scr[pl.ds(r0, CHUNK), j * LANES:(j + 1) * LANES] * mixed
            cat_scr[pl.ds(r0, CHUNK), Q_W + j * LANES:Q_W + (j + 1) * LANES] = out.astype(BF16)

    for bi in range(nblk):
        block(bi)


def _mixer_ab_call(x, pos, sinks, inv_lane, sgn_lane, sgu_g, sgu_b, w_s, bias_t, ln_g, ln_b,
                   alpha, li, ji, tm, seq, weights, casts):
    n, d = x.shape
    tiles = n // tm
    body = functools.partial(_mixer_ab_kernel, alpha, li, ji, tm, seq // tm, weights, casts)
    cur, prev = _staggered_maps(tiles)
    main_specs = [
        pl.BlockSpec(memory_space=pltpu.SMEM),
        pl.BlockSpec((tm, d), cur),
        pl.BlockSpec((tm, d), prev),
        pl.BlockSpec((tm // pos.shape[1], pos.shape[1]), cur),
        _resident((1, LANES)), _resident((1, LANES)),
        _resident(sgu_g.shape), _resident(sgu_b.shape),
        _resident(w_s.shape),
        _resident((CHUNK, SGU_W)),
        _resident((2 * WINDOW, 2 * WINDOW)),
        _resident(ln_g.shape), _resident(ln_b.shape),
    ]
    qi = jnp.arange(2 * WINDOW)[:, None] % WINDOW
    kj = jnp.arange(2 * WINDOW)[None, :]
    band = jnp.where((kj <= qi + WINDOW) & (kj > qi), 0.0, NEG_INF).astype(F32)
    main_args = [sinks, x, x, pos, inv_lane, sgn_lane, sgu_g, sgu_b, w_s, bias_t, band, ln_g, ln_b]
    scratch = [
        pltpu.VMEM((2, tm, Q_W), BF16),
        pltpu.VMEM((2, tm + WINDOW, 4 * LANES), BF16),
        pltpu.VMEM((2, tm, SGU_W), F32),
        pltpu.VMEM((2, tm, SGU_W), BF16),
        pltpu.VMEM((SGU_GROUPS // 2, CHUNK, 2 * CHUNK), BF16),
        pltpu.VMEM((tm, Q_W + SGU_W), BF16),
    ]
    return _call(body, "mixer_attn_sgu", tiles, tm, d, main_specs, main_args, weights, casts,
                 scratch)


def _mixer_conv_kernel(alpha, li, ji, tm, tiles_per_seq, weights, casts, *refs):
    main, (win_v, wout_v), load, o_ref, scratch = _split_refs(refs, 4, weights, casts)
    x_hbm, cw_ref, lng_ref, lnb_ref = main
    cz_scr, mix_keep, ring, ring_sem = scratch
    i = pl.program_id(0)
    tiles = pl.num_programs(0) - 1
    d = x_hbm.shape[1]
    pad = SUBLANES

    def fetch(t):
        rows = pl.ds(t * tm if isinstance(t, int) else pl.multiple_of(t * tm, tm), tm)
        return pltpu.make_async_copy(x_hbm.at[rows, :], ring.at[t % RING_SLOTS],
                                     ring_sem.at[t % RING_SLOTS])

    @pl.when(i == 0)
    def _prime():
        for t in range(RING_AHEAD):
            fetch(t).start()

    @pl.when(i + RING_AHEAD < tiles)
    def _prefetch():
        fetch(i + RING_AHEAD).start()

    @pl.when(i < tiles)
    def _arrive():
        fetch(i).wait()

    x_ref = ring.at[i % RING_SLOTS]
    xprev_ref = ring.at[(i + RING_SLOTS - 1) % RING_SLOTS]

    @pl.when((i % tiles_per_seq) == 0)
    def _reset_halo():
        cz_scr[0:pad, :] = jnp.zeros((pad, d), F32)

    def matmuls(zero):
        h = _dot(x_ref[...].astype(BF16), win_v[...])
        cz = h[:, d:2 * d] * h[:, 2 * d:3 * d]
        left = cz_scr[...]
        row = lax.broadcasted_iota(jnp.int32, (tm, 1), 0)
        y = (cw_ref[ji, CONV_WIDTH - 1:CONV_WIDTH, :] + zero) * cz
        for t in range(CONV_WIDTH - 1):
            back = CONV_WIDTH - 1 - t
            shifted = pltpu.roll(cz, back, axis=0)
            for r in range(back):
                shifted = jnp.where(row == r, left[pad - back + r:pad - back + r + 1, :], shifted)
            y = y + cw_ref[ji, t:t + 1, :] * shifted
        cz_scr[...] = cz[tm - pad:, :]
        mix_keep[...] = _dot((h[:, 0:d] * y).astype(BF16), wout_v[...])

    def finish(chained):
        def store(rows, y):
            o_ref[rows, :] = y
        return _residual_ln(alpha, 1.0, xprev_ref, mix_keep,
                            lng_ref[li, 1:2, :], lnb_ref[li, 1:2, :], store, chained)

    _staggered(load, matmuls, finish)


def _mixer_conv_call(x, conv_w, ln_g, ln_b, alpha, li, ji, tm, seq, weights, casts):
    n, d = x.shape
    tiles = n // tm
    body = functools.partial(_mixer_conv_kernel, alpha, li, ji, tm, seq // tm, weights, casts)
    assert tiles >= RING_AHEAD
    main_specs = [_HBM, _resident(conv_w.shape), _resident(ln_g.shape), _resident(ln_b.shape)]
    scratch = [pltpu.VMEM((SUBLANES, d), F32),
               pltpu.VMEM((tm, d), F32),
               pltpu.VMEM((RING_SLOTS, tm, d), F32),
               pltpu.SemaphoreType.DMA((RING_SLOTS,))]
    return _call(body, "mixer_conv", tiles, tm, d, main_specs, [x, conv_w, ln_g, ln_b],
                 weights, casts, scratch)


def kernel(x, p, positions, ln_g, ln_b, ffn_w_gate, ffn_w_up, ffn_w_down, ab_w_in, ab_sinks,
           sgu_ln_g, sgu_ln_b, sgu_w_s, sgu_b_s, ab_w_out, sc_w_in, sc_conv_w, sc_w_out,
           ple_w_proj, ple_w_gate):
    bsz, seq, d = x.shape
    depth = p.shape[0]
    alpha = (2 * depth) ** 0.25
    tm = TM
    assert seq % tm == 0 and tm % WINDOW == 0
    assert sc_conv_w.shape[1] == CONV_WIDTH

    n = bsz * seq
    steps = n // tm + 1
    xs = x.reshape(n, d)
    nseg = LANES // ROT_DIM
    pos = positions.reshape(n // tm, nseg, tm // nseg).transpose(0, 2, 1).reshape(n // nseg, nseg)
    ps = p.reshape(depth, n, p.shape[-1])

    half = ROT_DIM // 2
    r = jnp.arange(LANES) % ROT_DIM
    inv_lane = jnp.power(ROPE_THETA, -(r % half).astype(F32) * (2.0 / ROT_DIM)).reshape(1, LANES)
    sgn_lane = jnp.where(r < half, -1.0, 1.0).reshape(1, LANES).astype(F32)

    calls = []
    for i in range(depth):
        j = i // 2
        ffn = lambda h, i=i: [(ffn_w_gate, (i, h)), (ffn_w_up, (i, h)), (ffn_w_down, (i, h))]
        calls.append(("ffn", i, 0, ffn(0)))
        if i % 2 == 0:
            calls.append(("attn", i, j, [(ab_w_in, (j,)), (ab_w_out, (j,))]))
        else:
            calls.append(("conv", i, j, [(sc_w_in, (j,)), (sc_w_out, (j,))]))
        calls.append(("ple", i, 1, ffn(1) + [(ple_w_proj, (i,)), (ple_w_gate, (i,))]))

    given = None
    for k, (kind, i, j, mats) in enumerate(calls):
        weights = _Weights(mats, given)
        casts = _Casts(calls[k + 1][3] if k + 1 < len(calls) else [], steps)
        if kind == "ffn":
            xs, given = _ffn_call(xs, ln_g, ln_b, alpha, i, j, tm, weights, casts)
        elif kind == "ple":
            xs, given = _ffn_call(xs, ln_g, ln_b, alpha, i, j, tm, weights, casts, p=ps)
        elif kind == "attn":
            bias_t = jnp.repeat(jnp.transpose(sgu_b_s[j]), SGU_GROUP_DIM, axis=1)
            xs, given = _mixer_ab_call(xs, pos, ab_sinks, inv_lane, sgn_lane, sgu_ln_g, sgu_ln_b,
                                       sgu_w_s, bias_t, ln_g, ln_b, alpha, i, j, tm, seq,
                                       weights, casts)
        else:
            xs, given = _mixer_conv_call(xs, sc_conv_w, ln_g, ln_b, alpha, i, j, tm, seq,
                                         weights, casts)
    return xs.reshape(bsz, seq, d)
```

```python
import functools
import math

import jax
import jax.numpy as jnp
from jax import lax
from jax.experimental import pallas as pl
from jax.experimental.pallas import tpu as pltpu

ATT_HEADS = 8
ATT_KV_HEADS = 2
HEAD_DIM = 64
WINDOW = 128
ROT_DIM = HEAD_DIM // 4
ROPE_THETA = 500000.0
SGU_GROUPS = 8
SGU_GROUP_DIM = 64
CHUNK = 128
CONV_WIDTH = 3
LN_EPS = 1e-5
NEG_INF = -1e30

Q_W = ATT_HEADS * HEAD_DIM
KV_W = ATT_KV_HEADS * HEAD_DIM
SGU_W = SGU_GROUPS * SGU_GROUP_DIM

LANES = 128
SUBLANES = 8
BF16_ROWS = 16
VMEM_LIMIT = 56 * 1024 * 1024
STAGE_BYTES = 3 * 512 * 1024
STAGE_SLOTS = 6
TM = 512
EMBED_PARTS = 2
OUT_PARTS = 2
RING_AHEAD = 2
RING_SLOTS = RING_AHEAD + 2
LN_ROWS = 8
LN_AHEAD = 6

BF16 = jnp.bfloat16
F32 = jnp.float32


def _dot(a, b):
    return jnp.dot(a, b, preferred_element_type=F32)


def _layer_norm(y, g, b):
    mu = jnp.mean(y, axis=-1, keepdims=True)
    d = y - mu
    var = jnp.mean(d * d, axis=-1, keepdims=True)
    return d * lax.rsqrt(var + LN_EPS) * g + b


def _gelu_tanh(x):
    c = math.sqrt(2.0 / math.pi)
    half_x = 0.5 * x
    return half_x * jnp.tanh(x * ((c * 0.044715) * (x * x) + c)) + half_x


def _resident(shape):
    nd = len(shape)
    return pl.BlockSpec(shape, lambda i: (0,) * nd, pipeline_mode=pl.Buffered(1))


_HBM = pl.BlockSpec(memory_space=pl.ANY)


def _params():
    return pltpu.CompilerParams(dimension_semantics=("arbitrary",),
                                vmem_limit_bytes=VMEM_LIMIT)


def _stage(cols):
    rows = 1 << ((STAGE_BYTES // (4 * cols)).bit_length() - 1)
    return [pltpu.VMEM((STAGE_SLOTS, rows, cols), F32), pltpu.SemaphoreType.DMA((STAGE_SLOTS,))]


def _fetch_cast(jobs):
    chunks = []
    used = {}
    for src, dst, stage, sem in jobs:
        total, cols = src.shape
        rows = min(stage.shape[1], total)
        assert total % rows == 0 and cols == stage.shape[2] and dst.shape == src.shape
        for c in range(total // rows):
            slot = used.get(id(stage), 0) % STAGE_SLOTS
            used[id(stage)] = used.get(id(stage), 0) + 1
            staged = stage.at[slot, pl.ds(0, rows), :]
            copy = pltpu.make_async_copy(src.at[pl.ds(c * rows, rows), :], staged, sem.at[slot])
            chunks.append((copy, staged, dst.at[pl.ds(c * rows, rows), :]))
    for copy, _, _ in chunks[:STAGE_SLOTS]:
        copy.start()
    for n, (copy, staged, out) in enumerate(chunks):
        copy.wait()
        out[...] = staged[...].astype(BF16)
        if n + STAGE_SLOTS < len(chunks):
            chunks[n + STAGE_SLOTS][0].start()


class _Weights:
    def __init__(self, mats, given):
        self.mats, self.given = mats, given
        self.shapes = [tuple(a.shape[-2:]) for a, _ in mats]
        self.widths = sorted({s[1] for s in self.shapes})

    @property
    def args(self):
        return list(self.given) if self.given else [a for a, _ in self.mats]

    @property
    def in_specs(self):
        return [_resident(s) for s in self.shapes] if self.given else [_HBM] * len(self.mats)

    @property
    def scratch(self):
        if self.given:
            return []
        out = [pltpu.VMEM(s, BF16) for s in self.shapes]
        for w in self.widths:
            out += _stage(w)
        return out

    def bind(self, in_refs, scratch_refs):
        if self.given:
            return list(in_refs), (lambda: None)
        n = len(self.mats)
        stages = {w: (scratch_refs[n + 2 * k], scratch_refs[n + 2 * k + 1])
                  for k, w in enumerate(self.widths)}
        jobs = [(src.at[lead], dst) + stages[shape[1]]
                for src, (_, lead), dst, shape in zip(in_refs, self.mats, scratch_refs, self.shapes)]
        return list(scratch_refs[:n]), (lambda: _fetch_cast(jobs))


class _Casts:
    def __init__(self, mats, steps):
        self.mats = mats
        self.blocks = []
        for a, _ in mats:
            rows, cols = a.shape[-2:]
            rb = next(r for r in range(BF16_ROWS, rows + 1, BF16_ROWS)
                      if rows % r == 0 and rows // r <= steps)
            self.blocks.append((rb, rows // rb, rows, cols))

    @property
    def args(self):
        return [a for a, _ in self.mats]

    @property
    def in_specs(self):
        return [pl.BlockSpec((None,) * len(lead) + (rb, cols),
                             lambda i, lead=lead, nb=nb: lead + (jnp.minimum(i, nb - 1), 0))
                for (_, lead), (rb, nb, _, cols) in zip(self.mats, self.blocks)]

    @property
    def out_specs(self):
        return [pl.BlockSpec((rb, cols), lambda i, nb=nb: (jnp.minimum(i, nb - 1), 0))
                for rb, nb, _, cols in self.blocks]

    @property
    def out_shapes(self):
        return [jax.ShapeDtypeStruct((rows, cols), BF16) for _, _, rows, cols in self.blocks]

    @staticmethod
    def run(in_refs, out_refs):
        for src, dst in zip(in_refs, out_refs):
            dst[...] = src[...].astype(BF16)


def _split_refs(refs, n_main, weights, casts):
    nw, nc = len(weights.mats), len(casts.mats)
    main, rest = refs[:n_main], refs[n_main:]
    w_in, c_in = rest[:nw], rest[nw:nw + nc]
    o_ref, c_out = rest[nw + nc], rest[nw + nc + 1:nw + 2 * nc + 1]
    scratch = rest[nw + 2 * nc + 1:]
    ns = len(weights.scratch)
    w, load = weights.bind(w_in, scratch[:ns])
    _Casts.run(c_in, c_out)
    return main, w, load, o_ref, scratch[ns:]


def _call(body, name, tiles, tm, d, main_specs, main_args, weights, casts, scratch):
    outs = pl.pallas_call(
        body,
        grid=(tiles + 1,),
        in_specs=main_specs + weights.in_specs + casts.in_specs,
        out_specs=[pl.BlockSpec((tm, d), _staggered_maps(tiles)[1])] + casts.out_specs,
        out_shape=[jax.ShapeDtypeStruct((tiles * tm, d), F32)] + casts.out_shapes,
        scratch_shapes=weights.scratch + scratch,
        compiler_params=_params(),
        name=name,
    )(*main_args, *weights.args, *casts.args)
    return outs[0], list(outs[1:])


def _fold_bits(y):
    bits = lax.bitcast_convert_type(y, jnp.int32)
    cols = bits[:, 0:LANES]
    for j in range(1, y.shape[1] // LANES):
        cols = cols | bits[:, j * LANES:(j + 1) * LANES]
    rows = cols[0:SUBLANES]
    for k in range(1, y.shape[0] // SUBLANES):
        rows = rows | cols[k * SUBLANES:(k + 1) * SUBLANES]
    return rows


def _zero_from(bits):
    top = jnp.max(bits, axis=(0, 1), keepdims=True)
    cleared = lax.shift_right_logical(lax.shift_right_logical(top, 16), 16)
    return lax.bitcast_convert_type(cleared, F32)


def _residual_ln(alpha, beta, x_keep, r_keep, g, b, store, chained):
    if not chained:
        beta_r = r_keep[...] if beta == 1.0 else beta * r_keep[...]
        store(slice(None), _layer_norm(alpha * x_keep[...] + beta_r, g, b))
        return None
    ngroups = x_keep.shape[0] // LN_ROWS
    width = x_keep.shape[1]
    folded = []
    for c in range(ngroups):
        rows = pl.ds(c * LN_ROWS, LN_ROWS)
        a = alpha
        if c >= LN_AHEAD:
            z = lax.shift_right_logical(lax.shift_right_logical(folded[c - LN_AHEAD], 16), 16)
            a = alpha + jnp.tile(lax.bitcast_convert_type(z, F32),
                                 (LN_ROWS // SUBLANES, width // LANES))
        r = r_keep[rows, :]
        y = _layer_norm(a * x_keep[rows, :] + (r if beta == 1.0 else beta * r), g, b)
        store(rows, y)
        folded.append(_fold_bits(y))
    bits = folded[0]
    for f in folded[1:]:
        bits = bits | f
    return bits


def _staggered_maps(tiles):
    return (lambda i: (jnp.minimum(i, tiles - 1), 0)), (lambda i: (jnp.maximum(i - 1, 0), 0))


def _staggered(load, matmuls, finish, after=lambda: None):
    i = pl.program_id(0)
    last = pl.num_programs(0) - 1

    @pl.when(i == 0)
    def _first():
        load()
        matmuls(0.0)

    @pl.when(jnp.logical_and(i > 0, i < last))
    def _steady():
        matmuls(_zero_from(finish(True)))
        after()

    @pl.when(i == last)
    def _last():
        finish(False)
        after()


def _ffn_matmuls(x_ref, wg_v, wu_v, wd_v, ff_keep, zero):
    xb = x_ref[...].astype(BF16)
    gate = _dot(xb, wg_v[...])
    up = _dot(xb, wu_v[...])
    one = 1.0 + zero
    h = (gate * (one / (one + jnp.exp(-gate))) * up).astype(BF16)
    ff_keep[...] = _dot(h, wd_v[...])


def _ffn_kernel(alpha, li, hi, ple, weights, casts, *refs):
    main, w, load, o_ref, (ff_keep,) = _split_refs(refs, 5 if ple else 4, weights, casts)
    x_ref, xprev_ref, lng_ref, lnb_ref = main[:4]
    s = 2 * hi

    def store(rows, y):
        o_ref[rows, :] = y

    def finish(chained):
        return _residual_ln(alpha, 0.5, xprev_ref, ff_keep,
                            lng_ref[li, s:s + 1, :], lnb_ref[li, s:s + 1, :], store, chained)

    def embed():
        part = o_ref.shape[0] // EMBED_PARTS
        for k in range(EMBED_PARTS):
            rows = pl.ds(k * part, part)
            y = o_ref[rows, :]
            emb = _dot(main[4][rows, :].astype(BF16), w[3][...])
            gate = jax.nn.sigmoid(_dot(y.astype(BF16), w[4][...]))
            o_ref[rows, :] = y + emb * gate

    matmuls = functools.partial(_ffn_matmuls, x_ref, w[0], w[1], w[2], ff_keep)
    if ple:
        _staggered(load, matmuls, finish, embed)
    else:
        _staggered(load, matmuls, finish)


def _ffn_call(x, ln_g, ln_b, alpha, li, hi, tm, weights, casts, p=None):
    n, d = x.shape
    tiles = n // tm
    cur, prev = _staggered_maps(tiles)
    main_specs = [pl.BlockSpec((tm, d), cur), pl.BlockSpec((tm, d), prev),
                  _resident(ln_g.shape), _resident(ln_b.shape)]
    main_args = [x, x, ln_g, ln_b]
    if p is not None:
        main_specs.append(pl.BlockSpec((None, tm, p.shape[-1]), lambda i: (li,) + prev(i)))
        main_args.append(p)
    body = functools.partial(_ffn_kernel, alpha, li, hi, p is not None, weights, casts)
    scratch = [pltpu.VMEM((tm, d), F32)]
    return _call(body, "ffn_ln" if p is None else "ffn_ln_ple", tiles, tm, d,
                 main_specs, main_args, weights, casts, scratch)


def _mixer_ab_kernel(alpha, li, ji, tm, tiles_per_seq, weights, casts, *refs):
    main, (win_v, wout_v), load_weights, o_ref, scratch = _split_refs(refs, 13, weights, casts)
    (sinks_ref, x_ref, xprev_ref, pos_ref, inv_ref, sgn_ref, sg_ref, sb_ref, ws_ref, bs_ref,
     band_ref, lng_ref, lnb_ref) = main
    q_scr, kv_scr, u_scr, sv_scr, wcat_scr, cat_scr = scratch
    i = pl.program_id(0)
    last = pl.num_programs(0) - 1
    lane = lax.broadcasted_iota(jnp.int32, (1, LANES), 1)
    low_half = lane < HEAD_DIM
    slot = i % 2

    def load():
        load_weights()
        r = lax.broadcasted_iota(jnp.int32, (CHUNK, CHUNK), 0)
        c = lax.broadcasted_iota(jnp.int32, (CHUNK, CHUNK), 1)
        tril = r >= c
        for j in range(SGU_GROUPS // 2):
            a = jnp.where(tril, ws_ref[ji, 2 * j], 0.0)
            bb = jnp.where(tril, ws_ref[ji, 2 * j + 1], 0.0)
            wcat_scr[j] = jnp.concatenate([a, bb], axis=1).astype(BF16)

    def project():
        _mixer_ab_project(ji, tm, (i % tiles_per_seq) == 0, low_half, lane, x_ref, pos_ref,
                          win_v, inv_ref, sgn_ref, sg_ref, sb_ref,
                          q_scr.at[slot], kv_scr.at[slot], kv_scr.at[1 - slot],
                          u_scr.at[slot], sv_scr.at[slot])

    def attend():
        _mixer_ab_attend(ji, tm, ((i - 1) % tiles_per_seq) == 0, low_half, sinks_ref,
                         bs_ref, band_ref, q_scr.at[1 - slot], kv_scr.at[1 - slot],
                         u_scr.at[1 - slot], sv_scr.at[1 - slot], wcat_scr, cat_scr)
        part = tm // OUT_PARTS
        for k in range(OUT_PARTS):
            rows = pl.ds(k * part, part)
            mix = _dot(cat_scr[rows, :], wout_v[...])
            o_ref[rows, :] = _layer_norm(alpha * xprev_ref[rows, :] + mix,
                                         lng_ref[li, 1:2, :], lnb_ref[li, 1:2, :])

    @pl.when(i == 0)
    def _first():
        load()
        project()

    @pl.when(jnp.logical_and(i > 0, i < last))
    def _steady():
        attend()
        project()

    @pl.when(i == last)
    def _last():
        attend()


def _mixer_ab_project(ji, tm, first_tile, low_half, lane, x_ref, pos_ref, win_v,
                      inv_ref, sgn_ref, sg_ref, sb_ref, q_scr, kv_scr, kv_other, u_scr, sv_scr):
    x = x_ref[...]
    h = _dot(x.astype(BF16), win_v[...])

    nseg = LANES // ROT_DIM
    seg = tm // nseg
    pos = pos_ref[...].astype(F32)
    packed = jnp.zeros((seg, LANES), F32)
    for s in range(nseg):
        in_seg = (lane // ROT_DIM) == s
        packed = jnp.where(in_seg, pos[:, s:s + 1], packed)
    ang = packed * inv_ref[...]
    cos_p = jnp.cos(ang)
    sin_p = jnp.sin(ang) * sgn_ref[...]
    rot_lo = lane < ROT_DIM
    rot_hi = jnp.logical_and(lane >= HEAD_DIM, lane < HEAD_DIM + ROT_DIM)

    def unpack(t, fill):
        parts = []
        for s in range(nseg):
            lo = t if s == 0 else pltpu.roll(t, LANES - s * ROT_DIM, axis=1)
            shift = (HEAD_DIM - s * ROT_DIM) % LANES
            hi = t if shift == 0 else pltpu.roll(t, shift, axis=1)
            parts.append(jnp.where(rot_lo, lo, jnp.where(rot_hi, hi, fill)))
        return jnp.concatenate(parts, axis=0)

    cos_t = unpack(cos_p, 1.0)
    sin_t = unpack(sin_p, 0.0)
    take_up = (lane % HEAD_DIM) < (ROT_DIM // 2)

    def rotary(t):
        up = pltpu.roll(t, LANES - ROT_DIM // 2, axis=1)
        dn = pltpu.roll(t, ROT_DIM // 2, axis=1)
        return t * cos_t + jnp.where(take_up, up, dn) * sin_t

    scale = HEAD_DIM ** -0.5
    for j in range(Q_W // LANES):
        qj = rotary(h[:, j * LANES:(j + 1) * LANES]) * scale
        q_scr[:, j * LANES:(j + 1) * LANES] = qj.astype(BF16)

    def dup_heads(t):
        sw = pltpu.roll(t, HEAD_DIM, axis=1)
        return jnp.where(low_half, t, sw), jnp.where(low_half, sw, t)

    k0, k1 = dup_heads(rotary(h[:, Q_W:Q_W + KV_W]))
    v0, v1 = dup_heads(h[:, Q_W + KV_W:Q_W + 2 * KV_W])
    kv_scr[WINDOW:, 0 * LANES:1 * LANES] = k0.astype(BF16)
    kv_scr[WINDOW:, 1 * LANES:2 * LANES] = k1.astype(BF16)
    kv_scr[WINDOW:, 2 * LANES:3 * LANES] = v0.astype(BF16)
    kv_scr[WINDOW:, 3 * LANES:4 * LANES] = v1.astype(BF16)

    su0 = Q_W + 2 * KV_W
    u_scr[...] = _gelu_tanh(h[:, su0:su0 + SGU_W])
    sv = _layer_norm(_gelu_tanh(h[:, su0 + SGU_W:su0 + 2 * SGU_W]),
                     sg_ref[ji:ji + 1, :], sb_ref[ji:ji + 1, :])
    sv_scr[...] = sv.astype(BF16)

    halo = kv_other[tm:tm + WINDOW, :]
    kv_scr[0:WINDOW, :] = jnp.where(first_tile, jnp.zeros_like(halo), halo)


def _mixer_ab_attend(ji, tm, first_tile, low_half, sinks_ref, bs_ref, band_ref,
                     q_scr, kv_scr, u_scr, sv_scr, wcat_scr, cat_scr):
    nblk = tm // WINDOW
    row_first = lax.broadcasted_iota(jnp.int32, (2 * WINDOW, 1), 0) < WINDOW
    before_start = lax.broadcasted_iota(jnp.int32, (1, 2 * WINDOW), 1) < WINDOW
    first_bias = jnp.where(jnp.logical_and(first_tile, before_start), NEG_INF, band_ref[...])

    def block(bi):
        r0 = bi * WINDOW
        for kvh in range(ATT_KV_HEADS):
            kband = kv_scr[pl.ds(r0, 2 * WINDOW), kvh * LANES:(kvh + 1) * LANES]
            vband = kv_scr[pl.ds(r0, 2 * WINDOW), (2 + kvh) * LANES:(3 + kvh) * LANES]
            for pr in range(2):
                slab = kvh * 2 + pr
                qp = q_scr[pl.ds(r0, WINDOW), slab * LANES:(slab + 1) * LANES]
                blank = jnp.zeros_like(qp)
                qs = jnp.concatenate([jnp.where(low_half, qp, blank),
                                      jnp.where(low_half, blank, qp)], axis=0)
                s = lax.dot_general(qs, kband, (((1,), (1,)), ((), ())),
                                    preferred_element_type=F32)
                s = s + (first_bias if bi == 0 else band_ref[...])
                sink = jnp.where(row_first, sinks_ref[ji, 2 * slab], sinks_ref[ji, 2 * slab + 1])
                m = jnp.max(s, axis=-1, keepdims=True)
                p = jnp.exp(s - m)
                ov = _dot(p.astype(BF16), jnp.concatenate([vband, jnp.ones_like(vband)], axis=1))
                o = ov[:, :LANES] / (ov[:, LANES:] + jnp.exp(sink - m))
                att = jnp.where(low_half, o[:WINDOW], o[WINDOW:])
                cat_scr[pl.ds(r0, WINDOW), slab * LANES:(slab + 1) * LANES] = att.astype(BF16)
        for j in range(SGU_GROUPS // 2):
            vp = sv_scr[pl.ds(r0, CHUNK), j * LANES:(j + 1) * LANES]
            blank = jnp.zeros_like(vp)
            rhs = jnp.concatenate([jnp.where(low_half, vp, blank),
                                   jnp.where(low_half, blank, vp)], axis=0)
            mixed = _dot(wcat_scr[j], rhs) + bs_ref[:, j * LANES:(j + 1) * LANES]
            out = u_scr[pl.ds(r0, CHUNK), j * LANES:(j + 1) * LANES] * mixed
            cat_scr[pl.ds(r0, CHUNK), Q_W + j * LANES:Q_W + (j + 1) * LANES] = out.astype(BF16)

    for bi in range(nblk):
        block(bi)


def _mixer_ab_call(x, pos, sinks, inv_lane, sgn_lane, sgu_g, sgu_b, w_s, bias_t, ln_g, ln_b,
                   alpha, li, ji, tm, seq, weights, casts):
    n, d = x.shape
    tiles = n // tm
    body = functools.partial(_mixer_ab_kernel, alpha, li, ji, tm, seq // tm, weights, casts)
    cur, prev = _staggered_maps(tiles)
    main_specs = [
        pl.BlockSpec(memory_space=pltpu.SMEM),
        pl.BlockSpec((tm, d), cur),
        pl.BlockSpec((tm, d), prev),
        pl.BlockSpec((tm // pos.shape[1], pos.shape[1]), cur),
        _resident((1, LANES)), _resident((1, LANES)),
        _resident(sgu_g.shape), _resident(sgu_b.shape),
        _resident(w_s.shape),
        _resident((CHUNK, SGU_W)),
        _resident((2 * WINDOW, 2 * WINDOW)),
        _resident(ln_g.shape), _resident(ln_b.shape),
    ]
    qi = jnp.arange(2 * WINDOW)[:, None] % WINDOW
    kj = jnp.arange(2 * WINDOW)[None, :]
    band = jnp.where((kj <= qi + WINDOW) & (kj > qi), 0.0, NEG_INF).astype(F32)
    main_args = [sinks, x, x, pos, inv_lane, sgn_lane, sgu_g, sgu_b, w_s, bias_t, band, ln_g, ln_b]
    scratch = [
        pltpu.VMEM((2, tm, Q_W), BF16),
        pltpu.VMEM((2, tm + WINDOW, 4 * LANES), BF16),
        pltpu.VMEM((2, tm, SGU_W), F32),
        pltpu.VMEM((2, tm, SGU_W), BF16),
        pltpu.VMEM((SGU_GROUPS // 2, CHUNK, 2 * CHUNK), BF16),
        pltpu.VMEM((tm, Q_W + SGU_W), BF16),
    ]
    return _call(body, "mixer_attn_sgu", tiles, tm, d, main_specs, main_args, weights, casts,
                 scratch)


def _mixer_conv_kernel(alpha, li, ji, tm, tiles_per_seq, weights, casts, *refs):
    main, (win_v, wout_v), load, o_ref, scratch = _split_refs(refs, 4, weights, casts)
    x_hbm, cw_ref, lng_ref, lnb_ref = main
    cz_scr, mix_keep, ring, ring_sem = scratch
    i = pl.program_id(0)
    tiles = pl.num_programs(0) - 1
    d = x_hbm.shape[1]
    pad = SUBLANES

    def fetch(t):
        rows = pl.ds(t * tm if isinstance(t, int) else pl.multiple_of(t * tm, tm), tm)
        return pltpu.make_async_copy(x_hbm.at[rows, :], ring.at[t % RING_SLOTS],
                                     ring_sem.at[t % RING_SLOTS])

    @pl.when(i == 0)
    def _prime():
        for t in range(RING_AHEAD):
            fetch(t).start()

    @pl.when(i + RING_AHEAD < tiles)
    def _prefetch():
        fetch(i + RING_AHEAD).start()

    @pl.when(i < tiles)
    def _arrive():
        fetch(i).wait()

    x_ref = ring.at[i % RING_SLOTS]
    xprev_ref = ring.at[(i + RING_SLOTS - 1) % RING_SLOTS]

    @pl.when((i % tiles_per_seq) == 0)
    def _reset_halo():
        cz_scr[0:pad, :] = jnp.zeros((pad, d), F32)

    def matmuls(zero):
        h = _dot(x_ref[...].astype(BF16), win_v[...])
        cz = h[:, d:2 * d] * h[:, 2 * d:3 * d]
        left = cz_scr[...]
        row = lax.broadcasted_iota(jnp.int32, (tm, 1), 0)
        y = (cw_ref[ji, CONV_WIDTH - 1:CONV_WIDTH, :] + zero) * cz
        for t in range(CONV_WIDTH - 1):
            back = CONV_WIDTH - 1 - t
            shifted = pltpu.roll(cz, back, axis=0)
            for r in range(back):
                shifted = jnp.where(row == r, left[pad - back + r:pad - back + r + 1, :], shifted)
            y = y + cw_ref[ji, t:t + 1, :] * shifted
        cz_scr[...] = cz[tm - pad:, :]
        mix_keep[...] = _dot((h[:, 0:d] * y).astype(BF16), wout_v[...])

    def finish(chained):
        def store(rows, y):
            o_ref[rows, :] = y
        return _residual_ln(alpha, 1.0, xprev_ref, mix_keep,
                            lng_ref[li, 1:2, :], lnb_ref[li, 1:2, :], store, chained)

    _staggered(load, matmuls, finish)


def _mixer_conv_call(x, conv_w, ln_g, ln_b, alpha, li, ji, tm, seq, weights, casts):
    n, d = x.shape
    tiles = n // tm
    body = functools.partial(_mixer_conv_kernel, alpha, li, ji, tm, seq // tm, weights, casts)
    assert tiles >= RING_AHEAD
    main_specs = [_HBM, _resident(conv_w.shape), _resident(ln_g.shape), _resident(ln_b.shape)]
    scratch = [pltpu.VMEM((SUBLANES, d), F32),
               pltpu.VMEM((tm, d), F32),
               pltpu.VMEM((RING_SLOTS, tm, d), F32),
               pltpu.SemaphoreType.DMA((RING_SLOTS,))]
    return _call(body, "mixer_conv", tiles, tm, d, main_specs, [x, conv_w, ln_g, ln_b],
                 weights, casts, scratch)


def kernel(x, p, positions, ln_g, ln_b, ffn_w_gate, ffn_w_up, ffn_w_down, ab_w_in, ab_sinks,
           sgu_ln_g, sgu_ln_b, sgu_w_s, sgu_b_s, ab_w_out, sc_w_in, sc_conv_w, sc_w_out,
           ple_w_proj, ple_w_gate):
    bsz, seq, d = x.shape
    depth = p.shape[0]
    alpha = (2 * depth) ** 0.25
    tm = TM
    assert seq % tm == 0 and tm % WINDOW == 0
    assert sc_conv_w.shape[1] == CONV_WIDTH

    n = bsz * seq
    steps = n // tm + 1
    xs = x.reshape(n, d)
    nseg = LANES // ROT_DIM
    pos = positions.reshape(n // tm, nseg, tm // nseg).transpose(0, 2, 1).reshape(n // nseg, nseg)
    ps = p.reshape(depth, n, p.shape[-1])

    half = ROT_DIM // 2
    r = jnp.arange(LANES) % ROT_DIM
    inv_lane = jnp.power(ROPE_THETA, -(r % half).astype(F32) * (2.0 / ROT_DIM)).reshape(1, LANES)
    sgn_lane = jnp.where(r < half, -1.0, 1.0).reshape(1, LANES).astype(F32)

    calls = []
    for i in range(depth):
        j = i // 2
        ffn = lambda h, i=i: [(ffn_w_gate, (i, h)), (ffn_w_up, (i, h)), (ffn_w_down, (i, h))]
        calls.append(("ffn", i, 0, ffn(0)))
        if i % 2 == 0:
            calls.append(("attn", i, j, [(ab_w_in, (j,)), (ab_w_out, (j,))]))
        else:
            calls.append(("conv", i, j, [(sc_w_in, (j,)), (sc_w_out, (j,))]))
        calls.append(("ple", i, 1, ffn(1) + [(ple_w_proj, (i,)), (ple_w_gate, (i,))]))

    given = None
    for k, (kind, i, j, mats) in enumerate(calls):
        weights = _Weights(mats, given)
        casts = _Casts(calls[k + 1][3] if k + 1 < len(calls) else [], steps)
        if kind == "ffn":
            xs, given = _ffn_call(xs, ln_g, ln_b, alpha, i, j, tm, weights, casts)
        elif kind == "ple":
            xs, given = _ffn_call(xs, ln_g, ln_b, alpha, i, j, tm, weights, casts, p=ps)
        elif kind == "attn":
            bias_t = jnp.repeat(jnp.transpose(sgu_b_s[j]), SGU_GROUP_DIM, axis=1)
            xs, given = _mixer_ab_call(xs, pos, ab_sinks, inv_lane, sgn_lane, sgu_ln_g, sgu_ln_b,
                                       sgu_w_s, bias_t, ln_g, ln_b, alpha, i, j, tm, seq,
                                       weights, casts)
        else:
            xs, given = _mixer_conv_call(xs, sc_conv_w, ln_g, ln_b, alpha, i, j, tm // 2, seq,
                                         weights, casts)
    return xs.reshape(bsz, seq, d)
```
